```python
import jax, jax.numpy as jnp
from jax import lax
import numpy as np

D_MODEL = 1024
BATCH = 8
SEQ = 2048
DEPTH = 2

CHUNK = 64
Q_BLOCK = 128
BRANCH_WIDTH = D_MODEL // 2
HGRN_HEADS = 4
HGRN_HEAD_DIM = BRANCH_WIDTH // HGRN_HEADS
CONV_CH = BRANCH_WIDTH
CONV_WIDTH = 31
SB_HEADS = 8
SB_HEAD_DIM = BRANCH_WIDTH // SB_HEADS
N_BRANCH = 3
EPS = 1e-6
TINY = 1e-30
SPLIT_SIZES = (BRANCH_WIDTH, BRANCH_WIDTH, BRANCH_WIDTH, BRANCH_WIDTH,
               2 * CONV_CH, CONV_CH,
               BRANCH_WIDTH, BRANCH_WIDTH, BRANCH_WIDTH, BRANCH_WIDTH,
               N_BRANCH * D_MODEL)
IN_COLS = 11 * BRANCH_WIDTH + N_BRANCH * D_MODEL

kernel_name = "hybrid_gated_hgrn2_conformer_stickbreaking"


def rms_norm(x, w):
    xf = x.astype(jnp.float32)
    y = xf * lax.rsqrt(jnp.mean(xf * xf, axis=-1, keepdims=True) + EPS)
    return (y * w.astype(jnp.float32)).astype(x.dtype)


def layer_norm(x, w, b):
    xf = x.astype(jnp.float32)
    mu = jnp.mean(xf, axis=-1, keepdims=True)
    var = jnp.mean(jnp.square(xf - mu), axis=-1, keepdims=True)
    y = (xf - mu) * lax.rsqrt(var + EPS)
    return (y * w.astype(jnp.float32) + b.astype(jnp.float32)).astype(x.dtype)


def hgrn2_chunkwise(q, k, v, log_f):
    B, S, H, K = q.shape
    V = v.shape[-1]
    N = S // CHUNK

    def to_chunks(t):
        return t.reshape(B, N, CHUNK, H, t.shape[-1]).transpose(1, 0, 3, 2, 4)

    causal = jnp.tril(jnp.ones((CHUNK, CHUNK), dtype=bool))[None, None, :, :, None]

    def step(state, inp):
        q_c, k_c, v_c, g_c = inp
        b = jnp.cumsum(g_c, axis=2)
        b_last = b[:, :, -1:, :]
        o_inter = jnp.einsum('bhtk,bhkv->bhtv', q_c * jnp.exp(b), state)
        diff = b[:, :, :, None, :] - b[:, :, None, :, :]
        decay = jnp.where(causal, jnp.exp(jnp.where(causal, diff, 0.0)), 0.0)
        attn = jnp.einsum('bhtk,bhsk,bhtsk->bhts', q_c, k_c, decay)
        o = o_inter + jnp.einsum('bhts,bhsv->bhtv', attn, v_c)
        new_state = (jnp.exp(b_last[:, :, 0, :])[..., None] * state
                     + jnp.einsum('bhsk,bhsv->bhkv', k_c * jnp.exp(b_last - b), v_c))
        return new_state, o

    state0 = jnp.zeros((B, H, K, V), jnp.float32)
    _, o = lax.scan(step, state0, (to_chunks(q), to_chunks(k), to_chunks(v), to_chunks(log_f)))
    return o.transpose(1, 0, 3, 2, 4).reshape(B, S, H, V)


def hgrn2_branch(q_raw, f_raw, i_raw, z, lb, norm_w):
    B, S, _ = q_raw.shape
    shp = (B, S, HGRN_HEADS, HGRN_HEAD_DIM)
    q = jax.nn.silu(q_raw.astype(jnp.float32)) * (HGRN_HEAD_DIM ** -0.5)
    fr = f_raw.astype(jnp.float32)
    lbf = lb.astype(jnp.float32)
    f = lbf + (1.0 - lbf) * jax.nn.sigmoid(fr)
    log_f = jnp.log(jnp.maximum(f, TINY))
    k = (1.0 - lbf) * jax.nn.sigmoid(-fr)
    v = i_raw.astype(jnp.float32)
    o = hgrn2_chunkwise(q.reshape(shp), k.reshape(shp), v.reshape(shp), log_f.reshape(shp))
    o = rms_norm(o, norm_w) * jax.nn.silu(z.astype(jnp.float32).reshape(shp))
    return o.reshape(B, S, BRANCH_WIDTH).astype(q_raw.dtype)


def conformer_conv_branch(glu_in, z, conv_w, conv_b, ln_w, ln_b):
    a, g = jnp.split(glu_in, 2, axis=-1)
    u = a * jax.nn.sigmoid(g)
    y = lax.conv_general_dilated(u, conv_w[:, None, :].astype(u.dtype), window_strides=(1,),
                                 padding=((CONV_WIDTH - 1, 0),),
                                 dimension_numbers=('NWC', 'WIO', 'NWC'),
                                 feature_group_count=CONV_CH) + conv_b
    y = jax.nn.silu(layer_norm(y, ln_w, ln_b))
    return y * jax.nn.silu(z)


def stick_breaking_branch(q_raw, k_raw, v_raw, z):
    B, S, _ = q_raw.shape
    shp = (B, S, SB_HEADS, SB_HEAD_DIM)
    q = q_raw.astype(jnp.float32).reshape(shp)
    k = k_raw.astype(jnp.float32).reshape(shp)
    v = v_raw.astype(jnp.float32).reshape(shp)
    scale = SB_HEAD_DIM ** -0.5
    outs = []
    for blk in range(S // Q_BLOCK):
        t0, t1 = blk * Q_BLOCK, (blk + 1) * Q_BLOCK
        logits = jnp.einsum('bthd,bshd->bhts', q[:, t0:t1], k[:, :t1]) * scale
        mask = jnp.arange(t1)[None, :] < jnp.arange(t0, t1)[:, None]
        log_beta = jax.nn.log_sigmoid(logits)
        log_1m = jnp.where(mask, jax.nn.log_sigmoid(-logits), 0.0)
        after = lax.cumsum(log_1m, axis=3, reverse=True) - log_1m
        weights = jnp.where(mask, jnp.exp(jnp.where(mask, log_beta + after, 0.0)), 0.0)
        outs.append(jnp.einsum('bhts,bshd->bthd', weights, v[:, :t1]))
    o = jnp.concatenate(outs, axis=1).reshape(B, S, BRANCH_WIDTH)
    return (o * jax.nn.silu(z.astype(jnp.float32))).astype(q_raw.dtype)


def setup_inputs(seed: int = 0) -> dict:
    key = jax.random.key(seed)
    ks = jax.random.split(key, 16)
    D, W = D_MODEL, BRANCH_WIDTH
    f32 = jnp.float32
    nrm = lambda k, shp, s: jax.random.normal(k, shp, f32) * s
    return {
        "x": nrm(ks[0], (BATCH, SEQ, D), 1.0),
        "c": nrm(ks[1], (BATCH, D), 1.0),
        "ada_w": nrm(ks[2], (DEPTH, D, 3 * D), 0.5 * D ** -0.5),
        "ada_b": nrm(ks[3], (DEPTH, 3 * D), 0.02),
        "norm_w": 1.0 + nrm(ks[4], (DEPTH, D), 0.02),
        "w_in": nrm(ks[5], (DEPTH, D, IN_COLS), D ** -0.5),
        "hgrn_lb": nrm(ks[6], (DEPTH, W), 0.1),
        "hgrn_norm_w": 1.0 + nrm(ks[7], (DEPTH, HGRN_HEAD_DIM), 0.02),
        "conv_w": nrm(ks[8], (DEPTH, CONV_WIDTH, CONV_CH), CONV_WIDTH ** -0.5),
        "conv_b": nrm(ks[9], (DEPTH, CONV_CH), 0.02),
        "conv_ln_w": 1.0 + nrm(ks[10], (DEPTH, CONV_CH), 0.02),
        "conv_ln_b": nrm(ks[11], (DEPTH, CONV_CH), 0.02),
        "w_branch": nrm(ks[12], (DEPTH, N_BRANCH, W, D), W ** -0.5),
        "w_out": nrm(ks[13], (DEPTH, D, D), D ** -0.5),
        "final_norm_w": 1.0 + nrm(ks[14], (D,), 0.02),
    }


def reference(x, c, ada_w, ada_b, norm_w, w_in, hgrn_lb, hgrn_norm_w, conv_w, conv_b,
              conv_ln_w, conv_ln_b, w_branch, w_out, final_norm_w):
    B, S, D = x.shape
    lb_soft = jax.nn.softmax(hgrn_lb.astype(jnp.float32), axis=0)
    lower_bounds = jnp.cumsum(lb_soft, axis=0) - lb_soft[0:1]
    c_act = jax.nn.silu(c)
    split_at = [int(v) for v in np.cumsum(SPLIT_SIZES)[:-1]]
    for l in range(DEPTH):
        mod = c_act @ ada_w[l] + ada_b[l]
        shift, scale, gate = jnp.split(mod, 3, axis=-1)
        h = rms_norm(x, norm_w[l]) * (1.0 + scale[:, None, :]) + shift[:, None, :]
        proj = h @ w_in[l]
        (hq, hf, hi, hz, glu_in, cz, sq, sk, sv, sz, gate_logits) = jnp.split(proj, split_at, axis=-1)
        y_a = hgrn2_branch(hq, hf, hi, hz, lower_bounds[l], hgrn_norm_w[l])
        y_b = conformer_conv_branch(glu_in, cz, conv_w[l], conv_b[l], conv_ln_w[l], conv_ln_b[l])
        y_c = stick_breaking_branch(sq, sk, sv, sz)
        ys = jnp.stack([y_a, y_b, y_c.astype(y_a.dtype)], axis=0)
        branches = jnp.einsum('nbsw,nwd->nbsd', ys, w_branch[l])
        gates = jax.nn.sigmoid(gate_logits.astype(jnp.float32)).reshape(B, S, N_BRANCH, D).astype(x.dtype)
        merged = jnp.einsum('bsnd,nbsd->bsd', gates, branches)
        out = merged @ w_out[l]
        x = x + gate[:, None, :] * out
    return rms_norm(x, final_norm_w)
```

```python
import functools

import numpy as np
import jax
import jax.numpy as jnp
from jax import lax
from jax.experimental import pallas as pl
from jax.experimental.pallas import tpu as pltpu

F32 = jnp.float32
BF16 = jnp.bfloat16

LANES_V7X = 128
SUBLANES_V7X = 8
VMEM_BYTES_V7X = 64 * 1024 * 1024

EPS = 1e-6
TINY = 1e-30
BRANCH_WIDTH = 512
HGRN_HEADS = 4
HGRN_HEAD_DIM = BRANCH_WIDTH // HGRN_HEADS
HGRN_CHUNK = 64
HGRN_LEVELS = 6
CONV_WIDTH = 31
CONV_HALO = 32
SB_HEADS = 8
SB_HEAD_DIM = BRANCH_WIDTH // SB_HEADS
SB_BLOCK = 128
SB_SKIP_LOG = -104.0
N_BRANCH = 3


def _mib(n):
    return int(n) * 1024 * 1024


def _silu(x):
    return x * jax.nn.sigmoid(x)


def _dot(a, b):
    return jnp.dot(a.astype(BF16), b.astype(BF16), preferred_element_type=F32)


def _dot_nt(a, b):
    return lax.dot_general(a.astype(BF16), b.astype(BF16), (((1,), (1,)), ((), ())),
                           preferred_element_type=F32)


def _dot_tn(a, b):
    return lax.dot_general(a.astype(BF16), b.astype(BF16), (((0,), (0,)), ((), ())),
                           preferred_element_type=F32)


def _split3(x):
    hi = x.astype(BF16)
    r = x - hi.astype(F32)
    mid = r.astype(BF16)
    lo = (r - mid.astype(F32)).astype(BF16)
    return hi, mid, lo


def _dot_exact_lhs(c, x):
    hi, mid, lo = _split3(x)
    return (jnp.dot(c, hi, preferred_element_type=F32)
            + jnp.dot(c, mid, preferred_element_type=F32)
            + jnp.dot(c, lo, preferred_element_type=F32))


def _dot_exact_rhs(x, c):
    hi, mid, lo = _split3(x)
    return (jnp.dot(hi, c, preferred_element_type=F32)
            + jnp.dot(mid, c, preferred_element_type=F32)
            + jnp.dot(lo, c, preferred_element_type=F32))


def _ada_kernel(c_ref, w_ref, b_ref, o_ref):
    c = c_ref[...]
    ca = _silu(c)
    w = w_ref[...]
    ch = ca.astype(BF16)
    cl = (ca - ch.astype(F32)).astype(BF16)
    wh = w.astype(BF16)
    wl = (w - wh.astype(F32)).astype(BF16)
    acc = (jnp.dot(ch, wh, preferred_element_type=F32)
           + jnp.dot(ch, wl, preferred_element_type=F32)
           + jnp.dot(cl, wh, preferred_element_type=F32))
    o_ref[...] = acc + b_ref[...]


def _ada_mod(c, ada_w, ada_b):
    depth, d, n = ada_w.shape
    bsz = c.shape[0]
    tn = n // 4
    return pl.pallas_call(
        _ada_kernel,
        out_shape=jax.ShapeDtypeStruct((depth, bsz, n), F32),
        grid=(depth, n // tn),
        in_specs=[
            pl.BlockSpec((bsz, d), lambda l, j: (0, 0)),
            pl.BlockSpec((None, d, tn), lambda l, j: (l, 0, j)),
            pl.BlockSpec((None, 1, tn), lambda l, j: (l, 0, j)),
        ],
        out_specs=pl.BlockSpec((None, bsz, tn), lambda l, j: (l, 0, j)),
        compiler_params=pltpu.CompilerParams(
            dimension_semantics=("arbitrary", "arbitrary"),
            vmem_limit_bytes=_mib(32)),
        name="ada_mod",
    )(c, ada_w, ada_b.reshape(depth, 1, n))


_INPROJ_ROWS = 256


def _inproj_kernel(x_ref, shift_ref, scale_ref, nw_ref, w_ref, o_ref, h_scr):
    tm = x_ref.shape[0]
    n_chunks = tm // _INPROJ_ROWS

    @pl.when(pl.program_id(1) == 0)
    def _():
        gain = nw_ref[...] * (1.0 + scale_ref[...])
        shift = shift_ref[...]

        def body(i, carry):
            r0 = pl.multiple_of(i * _INPROJ_ROWS, _INPROJ_ROWS)
            x = x_ref[pl.ds(r0, _INPROJ_ROWS), :]
            ms = jnp.mean(x * x, axis=-1, keepdims=True)
            h = (x * lax.rsqrt(ms + EPS)) * gain + shift
            h_scr[pl.ds(r0, _INPROJ_ROWS), :] = h.astype(BF16)
            return carry

        lax.fori_loop(0, n_chunks, body, 0)

    def mm(i, carry):
        r0 = pl.multiple_of(i * _INPROJ_ROWS, _INPROJ_ROWS)
        o_ref[pl.ds(r0, _INPROJ_ROWS), :] = jnp.dot(
            h_scr[pl.ds(r0, _INPROJ_ROWS), :], w_ref[...], preferred_element_type=F32)
        return carry

    lax.fori_loop(0, n_chunks, mm, 0)


def _inproj(x2, mod5, norm_w, w_in_bf16, layer, seq):
    m, d = x2.shape
    n = w_in_bf16.shape[-1]
    tm = 1024
    tn = n // 4
    per_seq = seq // tm
    return pl.pallas_call(
        _inproj_kernel,
        out_shape=jax.ShapeDtypeStruct((m, n), F32),
        grid=(m // tm, n // tn),
        in_specs=[
            pl.BlockSpec((tm, d), lambda i, j: (i, 0)),
            pl.BlockSpec((None, None, None, 1, d), lambda i, j: (layer, i // per_seq, 0, 0, 0)),
            pl.BlockSpec((None, None, None, 1, d), lambda i, j: (layer, i // per_seq, 1, 0, 0)),
            pl.BlockSpec((None, 1, d), lambda i, j: (layer, 0, 0)),
            pl.BlockSpec((None, d, tn), lambda i, j: (layer, 0, j)),
        ],
        out_specs=pl.BlockSpec((tm, tn), lambda i, j: (i, j)),
        scratch_shapes=[pltpu.VMEM((tm, d), BF16)],
        compiler_params=pltpu.CompilerParams(
            dimension_semantics=("arbitrary", "arbitrary"),
            vmem_limit_bytes=_mib(48)),
        name="inproj",
    )(x2, mod5, mod5, norm_w.reshape(norm_w.shape[0], 1, d), w_in_bf16)


def _hgrn_constants():
    L = HGRN_CHUNK
    tri = np.tril(np.ones((L, L), np.float32))
    blocks = [tri]
    r = np.arange(L)
    for l in range(HGRN_LEVELS):
        h = 1 << l
        p = (r // (2 * h)) * (2 * h) + h - 1
        blocks.append(tri - tri[p])
    blocks.append(tri[L - 1][None, :] - tri)
    c_all = np.concatenate(blocks, axis=0)
    t = r[:, None]
    s = r[None, :]
    x = t ^ s
    lev = np.where(x > 0, np.floor(np.log2(np.maximum(x, 1))).astype(np.int32), HGRN_LEVELS)
    lev = np.where(s > t, -1, lev).astype(np.int32)
    return c_all, lev


def _hgrn_kernel(q_ref, f_ref, i_ref, z_ref, lb_ref, nw_ref, c_ref, lev_ref, o_ref, st_ref,
                 *, layer):
    L = HGRN_CHUNK
    K = HGRN_HEAD_DIM
    n_chunks = q_ref.shape[0] // L

    @pl.when(pl.program_id(1) == 0)
    def _():
        st_ref[...] = jnp.zeros_like(st_ref)

    lb_all = lb_ref[...]
    lb_exp = jnp.exp(lb_all - jnp.max(lb_all, axis=0, keepdims=True))
    lb_soft = lb_exp / jnp.sum(lb_exp, axis=0, keepdims=True)
    lower = jnp.zeros((1, lb_all.shape[1]), F32)
    for l in range(1, layer + 1):
        lower = lower + lb_soft[l:l + 1, :]
    one_m_lower = 1.0 - lower
    nw = nw_ref[...]

    def body(c, carry):
        r0 = pl.multiple_of(c * L, L)
        rows = pl.ds(r0, L)
        q = _silu(q_ref[rows, :]) * (K ** -0.5)
        fr = f_ref[rows, :]
        f = lower + one_m_lower * jax.nn.sigmoid(fr)
        g = jnp.log(jnp.maximum(f, TINY))
        k = one_m_lower * jax.nn.sigmoid(-fr)
        v = i_ref[rows, :]
        z = z_ref[rows, :]
        dall = _dot_exact_lhs(c_ref[...], g)
        lev = lev_ref[...]
        for h in range(HGRN_HEADS):
            cols = slice(h * K, (h + 1) * K)
            qh, kh, vh = q[:, cols], k[:, cols], v[:, cols]
            b = dall[0:L, cols]
            qe = qh * jnp.exp(b)
            kd = kh * jnp.exp(dall[(HGRN_LEVELS + 1) * L:(HGRN_LEVELS + 2) * L, cols])
            a = jnp.where(lev == HGRN_LEVELS, _dot_nt(qh, kh), 0.0)
            for l in range(HGRN_LEVELS):
                fac = jnp.exp(-jnp.abs(dall[(l + 1) * L:(l + 2) * L, cols]))
                a = a + jnp.where(lev == l, _dot_nt(qh * fac, kh * fac), 0.0)
            st = st_ref[h]
            o = _dot_nt(qe, st) + _dot(a, vh)
            st_ref[h] = st * jnp.exp(b[L - 1:L, :]) + _dot_tn(vh, kd)
            on = o * lax.rsqrt(jnp.mean(o * o, axis=-1, keepdims=True) + EPS) * nw
            o_ref[rows, cols] = (on * _silu(z[:, cols])).astype(o_ref.dtype)
        return carry

    lax.fori_loop(0, n_chunks, body, 0)


def _hgrn(proj3, hgrn_lb, hgrn_norm_w, layer):
    bsz, seq, _ = proj3.shape
    w = BRANCH_WIDTH
    tb = 512
    c_all, lev = _hgrn_constants()
    col = lambda cb: pl.BlockSpec((None, tb, w), lambda b, s: (b, s, cb))
    return pl.pallas_call(
        functools.partial(_hgrn_kernel, layer=layer),
        out_shape=jax.ShapeDtypeStruct((bsz, seq, w), BF16),
        grid=(bsz, seq // tb),
        in_specs=[
            col(0), col(1), col(2), col(3),
            pl.BlockSpec(hgrn_lb.shape, lambda b, s: (0, 0)),
            pl.BlockSpec((None, 1, HGRN_HEAD_DIM), lambda b, s: (layer, 0, 0)),
            pl.BlockSpec(c_all.shape, lambda b, s: (0, 0)),
            pl.BlockSpec(lev.shape, lambda b, s: (0, 0)),
        ],
        out_specs=pl.BlockSpec((None, tb, w), lambda b, s: (b, s, 0)),
        scratch_shapes=[pltpu.VMEM((HGRN_HEADS, HGRN_HEAD_DIM, HGRN_HEAD_DIM), F32)],
        compiler_params=pltpu.CompilerParams(
            dimension_semantics=("arbitrary", "arbitrary"),
            vmem_limit_bytes=_mib(32)),
        name="hgrn2",
    )(proj3, proj3, proj3, proj3, hgrn_lb,
      hgrn_norm_w.reshape(hgrn_norm_w.shape[0], 1, HGRN_HEAD_DIM),
      jnp.asarray(c_all, BF16), jnp.asarray(lev))


_CONV_ROWS = 32


def _conv_kernel(a_ref, g_ref, z_ref, w_ref, b_ref, lnw_ref, lnb_ref, o_ref, u_scr):
    ts = a_ref.shape[0]

    @pl.when(pl.program_id(1) == 0)
    def _():
        u_scr[0:CONV_HALO, :] = jnp.zeros((CONV_HALO, u_scr.shape[1]), F32)

    @pl.when(pl.program_id(1) > 0)
    def _():
        u_scr[0:CONV_HALO, :] = u_scr[ts:ts + CONV_HALO, :]

    u_scr[CONV_HALO:CONV_HALO + ts, :] = a_ref[...] * jax.nn.sigmoid(g_ref[...])

    bias = b_ref[...]
    lnw = lnw_ref[...]
    lnb = lnb_ref[...]
    first = CONV_HALO - (CONV_WIDTH - 1)
    for c in range(ts // _CONV_ROWS):
        r0 = c * _CONV_ROWS
        acc = jnp.broadcast_to(bias, (_CONV_ROWS, bias.shape[1]))
        for j in range(CONV_WIDTH):
            acc = acc + w_ref[j:j + 1, :] * u_scr[r0 + first + j:r0 + first + j + _CONV_ROWS, :]
        mu = jnp.mean(acc, axis=-1, keepdims=True)
        d = acc - mu
        var = jnp.mean(d * d, axis=-1, keepdims=True)
        y = _silu(d * lax.rsqrt(var + EPS) * lnw + lnb)
        o_ref[r0:r0 + _CONV_ROWS, :] = (y * _silu(z_ref[r0:r0 + _CONV_ROWS, :])).astype(o_ref.dtype)


def _conv(proj3, conv_w, conv_b, ln_w, ln_b, layer):
    bsz, seq, _ = proj3.shape
    w = BRANCH_WIDTH
    ts = 256
    col = lambda cb: pl.BlockSpec((None, ts, w), lambda b, s: (b, s, cb))
    vec = lambda: pl.BlockSpec((None, 1, w), lambda b, s: (layer, 0, 0))
    depth = conv_w.shape[0]
    return pl.pallas_call(
        _conv_kernel,
        out_shape=jax.ShapeDtypeStruct((bsz, seq, w), BF16),
        grid=(bsz, seq // ts),
        in_specs=[
            col(4), col(5), col(6),
            pl.BlockSpec((None, CONV_WIDTH, w), lambda b, s: (layer, 0, 0)),
            vec(), vec(), vec(),
        ],
        out_specs=pl.BlockSpec((None, ts, w), lambda b, s: (b, s, 0)),
        scratch_shapes=[pltpu.VMEM((CONV_HALO + ts, w), F32)],
        compiler_params=pltpu.CompilerParams(
            dimension_semantics=("arbitrary", "arbitrary"),
            vmem_limit_bytes=_mib(32)),
        name="conv_module",
    )(proj3, proj3, proj3, conv_w, conv_b.reshape(depth, 1, w), ln_w.reshape(depth, 1, w),
      ln_b.reshape(depth, 1, w))


def _sb_constants():
    n = SB_BLOCK
    u = (np.arange(n)[:, None] > np.arange(n)[None, :]).astype(np.float32)
    return np.concatenate([u, np.ones((n, n), np.float32)], axis=1)


def _sb_kernel(q_ref, k_ref, v_ref, z_ref, u_ref, o_ref):
    n = SB_BLOCK
    i = pl.program_id(2)
    q = q_ref[...] * (SB_HEAD_DIM ** -0.5)
    lane = lax.broadcasted_iota(jnp.int32, (n, n), 1)
    row = lax.broadcasted_iota(jnp.int32, (n, n), 0)
    head0 = lane < SB_HEAD_DIM
    qa = jnp.where(head0, q, 0.0).astype(BF16)
    qb = jnp.where(head0, 0.0, q).astype(BF16)
    u2 = u_ref[...]

    def block(qh, kj, vj, c, o, mask):
        x = _dot_nt(qh, kj)
        lp = jnp.log1p(jnp.exp(-jnp.abs(x)))
        log_beta = jnp.minimum(x, 0.0) - lp
        log_1m = jnp.minimum(-x, 0.0) - lp
        if mask is not None:
            log_1m = jnp.where(mask, log_1m, 0.0)
        r = _dot_exact_rhs(log_1m, u2)
        w = jnp.exp(log_beta + r[:, :n] + c)
        if mask is not None:
            w = jnp.where(mask, w, 0.0)
        return c + r[:, n:], o + _dot(w, vj)

    def kv(j):
        rows = pl.ds(pl.multiple_of(j * n, n), n)
        return k_ref[rows, :].astype(BF16), v_ref[rows, :].astype(BF16)

    kj, vj = kv(i)
    strict = lane < row
    zero = jnp.zeros((n, n), F32)
    ca, oa = block(qa, kj, vj, zero, zero, strict)
    cb, ob = block(qb, kj, vj, zero, zero, strict)

    def cond(carry):
        j, cmax = carry[0], carry[1]
        return jnp.logical_and(j >= 0, cmax > SB_SKIP_LOG)

    def body(carry):
        j, _, ca, oa, cb, ob = carry
        kj, vj = kv(j)
        ca, oa = block(qa, kj, vj, ca, oa, None)
        cb, ob = block(qb, kj, vj, cb, ob, None)
        cmax = jnp.maximum(jnp.max(ca), jnp.max(cb))
        return j - 1, cmax, ca, oa, cb, ob

    cmax0 = jnp.maximum(jnp.max(ca), jnp.max(cb))
    _, _, ca, oa, cb, ob = lax.while_loop(cond, body, (i - 1, cmax0, ca, oa, cb, ob))
    o = jnp.where(head0, oa, ob)
    o_ref[...] = (o * _silu(z_ref[...])).astype(o_ref.dtype)


def _sb_attn(proj3):
    bsz, seq, _ = proj3.shape
    n = SB_BLOCK
    pairs = BRANCH_WIDTH // LANES_V7X
    base = (7 * BRANCH_WIDTH) // LANES_V7X
    u2 = _sb_constants()
    return pl.pallas_call(
        _sb_kernel,
        out_shape=jax.ShapeDtypeStruct((bsz, seq, BRANCH_WIDTH), BF16),
        grid=(bsz, pairs, seq // n),
        in_specs=[
            pl.BlockSpec((None, n, LANES_V7X), lambda b, p, i: (b, i, base + p)),
            pl.BlockSpec((None, seq, LANES_V7X), lambda b, p, i: (b, 0, base + pairs + p)),
            pl.BlockSpec((None, seq, LANES_V7X), lambda b, p, i: (b, 0, base + 2 * pairs + p)),
            pl.BlockSpec((None, n, LANES_V7X), lambda b, p, i: (b, i, base + 3 * pairs + p)),
            pl.BlockSpec(u2.shape, lambda b, p, i: (0, 0)),
        ],
        out_specs=pl.BlockSpec((None, n, LANES_V7X), lambda b, p, i: (b, i, p)),
        compiler_params=pltpu.CompilerParams(
            dimension_semantics=("arbitrary", "arbitrary", "arbitrary"),
            vmem_limit_bytes=_mib(32)),
        name="stick_breaking",
    )(proj3, proj3, proj3, proj3, jnp.asarray(u2, BF16))


def _merge_kernel(ya_ref, yb_ref, yc_ref, g0, g1, g2, g3, g4, g5, x_ref, gate_ref, wb_ref,
                  wo_ref, fnw_ref, o_ref, *, final):
    w = BRANCH_WIDTH
    ys = (ya_ref[...], yb_ref[...], yc_ref[...])
    gl = ((g0, g1), (g2, g3), (g4, g5))
    halves = []
    for half in range(2):
        acc = None
        for nb in range(N_BRANCH):
            br = jnp.dot(ys[nb], wb_ref[nb, :, half * w:(half + 1) * w],
                         preferred_element_type=F32)
            term = jax.nn.sigmoid(gl[nb][half][...]) * br
            acc = term if acc is None else acc + term
        halves.append(acc.astype(BF16))
    merged = jnp.concatenate(halves, axis=1)
    out = jnp.dot(merged, wo_ref[...], preferred_element_type=F32)
    xn = x_ref[...] + gate_ref[...] * out
    if final:
        ms = jnp.mean(xn * xn, axis=-1, keepdims=True)
        xn = xn * lax.rsqrt(ms + EPS) * fnw_ref[...]
    o_ref[...] = xn


def _merge(ya, yb, yc, proj, x2, mod5, wb_bf16, wo_bf16, final_norm_w, layer, seq, final):
    m, d = x2.shape
    w = BRANCH_WIDTH
    tm = 512
    per_seq = seq // tm
    gate0 = 11
    ycol = lambda: pl.BlockSpec((tm, w), lambda i: (i, 0))
    gcol = lambda cb: pl.BlockSpec((tm, w), lambda i: (i, gate0 + cb))
    return pl.pallas_call(
        functools.partial(_merge_kernel, final=final),
        out_shape=jax.ShapeDtypeStruct((m, d), F32),
        grid=(m // tm,),
        in_specs=[
            ycol(), ycol(), ycol(),
            gcol(0), gcol(1), gcol(2), gcol(3), gcol(4), gcol(5),
            pl.BlockSpec((tm, d), lambda i: (i, 0)),
            pl.BlockSpec((None, None, None, 1, d), lambda i: (layer, i // per_seq, 2, 0, 0)),
            pl.BlockSpec((None, N_BRANCH, w, d), lambda i: (layer, 0, 0, 0)),
            pl.BlockSpec((None, d, d), lambda i: (layer, 0, 0)),
            pl.BlockSpec((1, d), lambda i: (0, 0)),
        ],
        out_specs=pl.BlockSpec((tm, d), lambda i: (i, 0)),
        compiler_params=pltpu.CompilerParams(
            dimension_semantics=("arbitrary",),
            vmem_limit_bytes=_mib(48)),
        name="merge_out",
    )(ya, yb, yc, proj, proj, proj, proj, proj, proj, x2, mod5, wb_bf16, wo_bf16,
      final_norm_w.reshape(1, d))


def kernel(x, c, ada_w, ada_b, norm_w, w_in, hgrn_lb, hgrn_norm_w, conv_w, conv_b, conv_ln_w,
           conv_ln_b, w_branch, w_out, final_norm_w):
    bsz, seq, d = x.shape
    depth = ada_w.shape[0]
    m = bsz * seq
    mod = _ada_mod(c, ada_w, ada_b)
    mod5 = mod.reshape(depth, bsz, 3, 1, d)
    w_in_b = w_in.astype(BF16)
    wb_b = w_branch.astype(BF16)
    wo_b = w_out.astype(BF16)
    x2 = x.reshape(m, d)
    for layer in range(depth):
        proj = _inproj(x2, mod5, norm_w, w_in_b, layer, seq)
        proj3 = proj.reshape(bsz, seq, proj.shape[-1])
        ya = _hgrn(proj3, hgrn_lb, hgrn_norm_w, layer)
        yb = _conv(proj3, conv_w, conv_b, conv_ln_w, conv_ln_b, layer)
        yc = _sb_attn(proj3)
        x2 = _merge(ya.reshape(m, -1), yb.reshape(m, -1), yc.reshape(m, -1), proj, x2, mod5,
                    wb_b, wo_b, final_norm_w, layer, seq, final=(layer == depth - 1))
    return x2.reshape(bsz, seq, d)
```

```python
import functools

import numpy as np
import jax
import jax.numpy as jnp
from jax import lax
from jax.experimental import pallas as pl
from jax.experimental.pallas import tpu as pltpu

F32 = jnp.float32
BF16 = jnp.bfloat16

LANES_V7X = 128
SUBLANES_V7X = 8
VMEM_BYTES_V7X = 64 * 1024 * 1024

EPS = 1e-6
TINY = 1e-30
LOG2E = 1.4426950408889634
BRANCH_WIDTH = 512
HGRN_HEADS = 4
HGRN_HEAD_DIM = BRANCH_WIDTH // HGRN_HEADS
HGRN_CHUNK = 64
HGRN_LEVELS = 6
_HGRN_MXU_LEVELS = 3
CONV_WIDTH = 31
CONV_HALO = 32
SB_HEADS = 8
SB_HEAD_DIM = BRANCH_WIDTH // SB_HEADS
SB_BLOCK = 128
SB_SKIP_LOG = -104.0
N_BRANCH = 3


def _mib(n):
    return int(n) * 1024 * 1024


def _silu(x):
    return x * jax.nn.sigmoid(x)


def _dot(a, b):
    return jnp.dot(a.astype(BF16), b.astype(BF16), preferred_element_type=F32)


def _dot_nt(a, b):
    return lax.dot_general(a.astype(BF16), b.astype(BF16), (((1,), (1,)), ((), ())),
                           preferred_element_type=F32)


def _dot_tn(a, b):
    return lax.dot_general(a.astype(BF16), b.astype(BF16), (((0,), (0,)), ((), ())),
                           preferred_element_type=F32)


def _split3(x):
    hi = x.astype(BF16)
    r = x - hi.astype(F32)
    mid = r.astype(BF16)
    lo = (r - mid.astype(F32)).astype(BF16)
    return hi, mid, lo


def _ada_kernel(c_ref, w_ref, b_ref, o_ref):
    c = c_ref[...]
    ca = _silu(c)
    w = w_ref[...]
    ch = ca.astype(BF16)
    cl = (ca - ch.astype(F32)).astype(BF16)
    wh = w.astype(BF16)
    wl = (w - wh.astype(F32)).astype(BF16)
    acc = (jnp.dot(ch, wh, preferred_element_type=F32)
           + jnp.dot(ch, wl, preferred_element_type=F32)
           + jnp.dot(cl, wh, preferred_element_type=F32))
    o_ref[...] = acc + b_ref[...]


def _ada_mod(c, ada_w, ada_b):
    depth, d, n = ada_w.shape
    bsz = c.shape[0]
    tn = n // 4
    return pl.pallas_call(
        _ada_kernel,
        out_shape=jax.ShapeDtypeStruct((depth, bsz, n), F32),
        grid=(depth, n // tn),
        in_specs=[
            pl.BlockSpec((bsz, d), lambda l, j: (0, 0)),
            pl.BlockSpec((None, d, tn), lambda l, j: (l, 0, j)),
            pl.BlockSpec((None, 1, tn), lambda l, j: (l, 0, j)),
        ],
        out_specs=pl.BlockSpec((None, bsz, tn), lambda l, j: (l, 0, j)),
        compiler_params=pltpu.CompilerParams(
            dimension_semantics=("arbitrary", "arbitrary"),
            vmem_limit_bytes=_mib(32)),
        name="ada_mod",
    )(c, ada_w, ada_b.reshape(depth, 1, n))


_INPROJ_ROWS = 256


def _inproj_kernel(x_ref, shift_ref, scale_ref, nw_ref, w_ref, o_ref, h_scr):
    tm = x_ref.shape[0]
    n_chunks = tm // _INPROJ_ROWS

    @pl.when(pl.program_id(1) == 0)
    def _():
        gain = nw_ref[...] * (1.0 + scale_ref[...])
        shift = shift_ref[...]

        def body(i, carry):
            r0 = pl.multiple_of(i * _INPROJ_ROWS, _INPROJ_ROWS)
            x = x_ref[pl.ds(r0, _INPROJ_ROWS), :]
            ms = jnp.mean(x * x, axis=-1, keepdims=True)
            h = (x * lax.rsqrt(ms + EPS)) * gain + shift
            h_scr[pl.ds(r0, _INPROJ_ROWS), :] = h.astype(BF16)
            return carry

        lax.fori_loop(0, n_chunks, body, 0)

    def mm(i, carry):
        r0 = pl.multiple_of(i * _INPROJ_ROWS, _INPROJ_ROWS)
        o_ref[pl.ds(r0, _INPROJ_ROWS), :] = jnp.dot(
            h_scr[pl.ds(r0, _INPROJ_ROWS), :], w_ref[...], preferred_element_type=F32)
        return carry

    lax.fori_loop(0, n_chunks, mm, 0)


def _inproj(x2, mod5, norm_w, w_in_bf16, layer, seq):
    m, d = x2.shape
    n = w_in_bf16.shape[-1]
    tm = 1024
    tn = n // 4
    per_seq = seq // tm
    return pl.pallas_call(
        _inproj_kernel,
        out_shape=jax.ShapeDtypeStruct((m, n), F32),
        grid=(m // tm, n // tn),
        in_specs=[
            pl.BlockSpec((tm, d), lambda i, j: (i, 0)),
            pl.BlockSpec((None, None, None, 1, d), lambda i, j: (layer, i // per_seq, 0, 0, 0)),
            pl.BlockSpec((None, None, None, 1, d), lambda i, j: (layer, i // per_seq, 1, 0, 0)),
            pl.BlockSpec((None, 1, d), lambda i, j: (layer, 0, 0)),
            pl.BlockSpec((None, d, tn), lambda i, j: (layer, 0, j)),
        ],
        out_specs=pl.BlockSpec((tm, tn), lambda i, j: (i, j)),
        scratch_shapes=[pltpu.VMEM((tm, d), BF16)],
        compiler_params=pltpu.CompilerParams(
            dimension_semantics=("arbitrary", "arbitrary"),
            vmem_limit_bytes=_mib(48)),
        name="inproj",
    )(x2, mod5, mod5, norm_w.reshape(norm_w.shape[0], 1, d), w_in_bf16)


def _hgrn_constants():
    L = HGRN_CHUNK
    tri = np.tril(np.ones((L, L), np.float32))
    blocks = [tri]
    r = np.arange(L)
    for l in range(_HGRN_MXU_LEVELS):
        h = 1 << l
        p = (r // (2 * h)) * (2 * h) + h - 1
        blocks.append(tri - tri[p])
    c_all = np.concatenate(blocks, axis=0)
    c_all = np.concatenate([c_all, c_all, c_all], axis=1)
    t = r[:, None]
    s = r[None, :]
    x = t ^ s
    lev = np.where(x > 0, np.floor(np.log2(np.maximum(x, 1))).astype(np.int32), HGRN_LEVELS)
    lev = np.where(s > t, -1, lev).astype(np.int32)
    return c_all, lev


def _hgrn_kernel(q_ref, f_ref, i_ref, z_ref, lb_ref, nw_ref, c_ref, lev_ref, o_ref, st_ref,
                 *, layer):
    L = HGRN_CHUNK
    K = HGRN_HEAD_DIM
    n_chunks = q_ref.shape[0] // L

    @pl.when(pl.program_id(1) == 0)
    def _():
        st_ref[...] = jnp.zeros_like(st_ref)

    lb_all = lb_ref[...]
    lb_exp = jnp.exp(lb_all - jnp.max(lb_all, axis=0, keepdims=True))
    lb_soft = lb_exp / jnp.sum(lb_exp, axis=0, keepdims=True)
    lower = jnp.zeros((1, lb_all.shape[1]), F32)
    for l in range(1, layer + 1):
        lower = lower + lb_soft[l:l + 1, :]
    one_m_lower = 1.0 - lower
    nw = nw_ref[...]

    def body(c, carry):
        r0 = pl.multiple_of(c * L, L)
        rows = pl.ds(r0, L)
        q = _silu(q_ref[rows, :]) * (K ** -0.5)
        fr = f_ref[rows, :]
        f = lower + one_m_lower * jax.nn.sigmoid(fr)
        g = jnp.log(jnp.maximum(f, TINY)) * LOG2E
        k = one_m_lower * jax.nn.sigmoid(-fr)
        v = i_ref[rows, :]
        z = z_ref[rows, :]
        dall = jnp.dot(c_ref[...], jnp.concatenate(_split3(g), axis=0),
                       preferred_element_type=F32)
        lev = lev_ref[...]
        heads = range(HGRN_HEADS)
        hcols = [slice(h * K, (h + 1) * K) for h in heads]

        def boundary_diff(b, l, cols):
            if l < _HGRN_MXU_LEVELS:
                return dall[(l + 1) * L:(l + 2) * L, cols]
            hw = 1 << l
            parts = [jnp.broadcast_to(b[s + hw - 1:s + hw, :], (2 * hw, K))
                     for s in range(0, L, 2 * hw)]
            return b - (parts[0] if len(parts) == 1 else jnp.concatenate(parts, axis=0))

        pair, o_inter, upd, decay = [], [], [], []
        for h in heads:
            cols = hcols[h]
            qh, kh, vh = q[:, cols], k[:, cols], v[:, cols]
            b = dall[0:L, cols]
            b_last = b[L - 1:L, :]
            o_inter.append(_dot_nt(qh * jnp.exp2(b), st_ref[h]))
            upd.append(_dot_tn(vh, kh * jnp.exp2(b_last - b)))
            decay.append(jnp.exp2(b_last))
            prods = [_dot_nt(qh, kh)]
            for l in range(HGRN_LEVELS):
                fac = jnp.exp2(-jnp.abs(boundary_diff(b, l, cols)))
                prods.append(_dot_nt(qh * fac, kh * fac))
            pair.append(prods)
        outs = []
        for h in heads:
            a = jnp.where(lev == HGRN_LEVELS, pair[h][0], 0.0)
            for l in range(HGRN_LEVELS):
                a = jnp.where(lev == l, pair[h][l + 1], a)
            outs.append(o_inter[h] + _dot(a, v[:, hcols[h]]))
        for h in heads:
            st_ref[h] = st_ref[h] * decay[h] + upd[h]
            o = outs[h]
            on = o * lax.rsqrt(jnp.mean(o * o, axis=-1, keepdims=True) + EPS) * nw
            o_ref[rows, hcols[h]] = (on * _silu(z[:, hcols[h]])).astype(o_ref.dtype)
        return carry

    lax.fori_loop(0, n_chunks, body, 0, unroll=2)


def _hgrn(proj3, hgrn_lb, hgrn_norm_w, layer):
    bsz, seq, _ = proj3.shape
    w = BRANCH_WIDTH
    tb = 512
    c_all, lev = _hgrn_constants()
    col = lambda cb: pl.BlockSpec((None, tb, w), lambda b, s: (b, s, cb))
    return pl.pallas_call(
        functools.partial(_hgrn_kernel, layer=layer),
        out_shape=jax.ShapeDtypeStruct((bsz, seq, w), BF16),
        grid=(bsz, seq // tb),
        in_specs=[
            col(0), col(1), col(2), col(3),
            pl.BlockSpec(hgrn_lb.shape, lambda b, s: (0, 0)),
            pl.BlockSpec((None, 1, HGRN_HEAD_DIM), lambda b, s: (layer, 0, 0)),
            pl.BlockSpec(c_all.shape, lambda b, s: (0, 0)),
            pl.BlockSpec(lev.shape, lambda b, s: (0, 0)),
        ],
        out_specs=pl.BlockSpec((None, tb, w), lambda b, s: (b, s, 0)),
        scratch_shapes=[pltpu.VMEM((HGRN_HEADS, HGRN_HEAD_DIM, HGRN_HEAD_DIM), F32)],
        compiler_params=pltpu.CompilerParams(
            dimension_semantics=("arbitrary", "arbitrary"),
            vmem_limit_bytes=_mib(32)),
        name="hgrn2",
    )(proj3, proj3, proj3, proj3, hgrn_lb,
      hgrn_norm_w.reshape(hgrn_norm_w.shape[0], 1, HGRN_HEAD_DIM),
      jnp.asarray(c_all, BF16), jnp.asarray(lev))


_CONV_ROWS = 32


def _conv_kernel(a_ref, g_ref, z_ref, w_ref, b_ref, lnw_ref, lnb_ref, o_ref, u_scr):
    ts = a_ref.shape[0]

    @pl.when(pl.program_id(1) == 0)
    def _():
        u_scr[0:CONV_HALO, :] = jnp.zeros((CONV_HALO, u_scr.shape[1]), F32)

    @pl.when(pl.program_id(1) > 0)
    def _():
        u_scr[0:CONV_HALO, :] = u_scr[ts:ts + CONV_HALO, :]

    u_scr[CONV_HALO:CONV_HALO + ts, :] = a_ref[...] * jax.nn.sigmoid(g_ref[...])

    bias = b_ref[...]
    lnw = lnw_ref[...]
    lnb = lnb_ref[...]
    first = CONV_HALO - (CONV_WIDTH - 1)
    sub = SUBLANES_V7X
    for c in range(ts // _CONV_ROWS):
        r0 = c * _CONV_ROWS
        acc = jnp.broadcast_to(bias, (_CONV_ROWS, bias.shape[1]))
        for res in range(sub):
            taps = [j for j in range(CONV_WIDTH) if (first + j) % sub == res]
            rows = _CONV_ROWS + (sub if res else 0)
            part = None
            for j in taps:
                a0 = r0 + first + j - res
                term = w_ref[j:j + 1, :] * u_scr[a0:a0 + rows, :]
                part = term if part is None else part + term
            acc = acc + part[res:res + _CONV_ROWS, :]
        mu = jnp.mean(acc, axis=-1, keepdims=True)
        d = acc - mu
        var = jnp.mean(d * d, axis=-1, keepdims=True)
        y = _silu(d * lax.rsqrt(var + EPS) * lnw + lnb)
        o_ref[r0:r0 + _CONV_ROWS, :] = (y * _silu(z_ref[r0:r0 + _CONV_ROWS, :])).astype(o_ref.dtype)


def _conv(proj3, conv_w, conv_b, ln_w, ln_b, layer):
    bsz, seq, _ = proj3.shape
    w = BRANCH_WIDTH
    ts = 256
    col = lambda cb: pl.BlockSpec((None, ts, w), lambda b, s: (b, s, cb))
    vec = lambda: pl.BlockSpec((None, 1, w), lambda b, s: (layer, 0, 0))
    depth = conv_w.shape[0]
    return pl.pallas_call(
        _conv_kernel,
        out_shape=jax.ShapeDtypeStruct((bsz, seq, w), BF16),
        grid=(bsz, seq // ts),
        in_specs=[
            col(4), col(5), col(6),
            pl.BlockSpec((None, CONV_WIDTH, w), lambda b, s: (layer, 0, 0)),
            vec(), vec(), vec(),
        ],
        out_specs=pl.BlockSpec((None, ts, w), lambda b, s: (b, s, 0)),
        scratch_shapes=[pltpu.VMEM((CONV_HALO + ts, w), F32)],
        compiler_params=pltpu.CompilerParams(
            dimension_semantics=("arbitrary", "arbitrary"),
            vmem_limit_bytes=_mib(32)),
        name="conv_module",
    )(proj3, proj3, proj3, conv_w, conv_b.reshape(depth, 1, w), ln_w.reshape(depth, 1, w),
      ln_b.reshape(depth, 1, w))


def _sb_constants():
    n = SB_BLOCK
    u = (np.arange(n)[:, None] > np.arange(n)[None, :]).astype(np.float32)
    u1 = np.concatenate([u, np.ones((n, n), np.float32)], axis=1)
    return np.concatenate([u1, u1], axis=0)


def _sb_kernel(q_ref, k_ref, v_ref, z_ref, u_ref, o_ref, q_scr, hl_scr, c_scr, o_scr):
    n = SB_BLOCK
    pairs = q_ref.shape[1] // n
    i = pl.program_id(1)
    lane = lax.broadcasted_iota(jnp.int32, (2 * n, n), 1)
    row = lax.broadcasted_iota(jnp.int32, (2 * n, n), 0)
    strict = lane < (row & (n - 1))
    head_a = lax.broadcasted_iota(jnp.int32, (n, n), 1) < SB_HEAD_DIM

    for p in range(pairs):
        q = q_ref[:, p * n:(p + 1) * n] * (SB_HEAD_DIM ** -0.5)
        q_scr[p, 0:n, :] = jnp.where(head_a, q, 0.0).astype(BF16)
        q_scr[p, n:2 * n, :] = jnp.where(head_a, 0.0, q).astype(BF16)

    def visit(j, diag):
        rows = pl.ds(pl.multiple_of(j * n, n), n)
        prs = range(pairs)
        kj = [k_ref[rows, p * n:(p + 1) * n].astype(BF16) for p in prs]
        vj = [v_ref[rows, p * n:(p + 1) * n].astype(BF16) for p in prs]
        logits = [_dot_nt(q_scr[p], kj[p]) for p in prs]
        log_beta, sums = [], []
        for p in prs:
            x = logits[p]
            lp = jnp.log(1.0 + jnp.exp2(jnp.abs(x) * (-LOG2E)))
            log_beta.append(jnp.minimum(x, 0.0) - lp)
            drop = jnp.maximum(x, 0.0) + lp
            if diag:
                drop = jnp.where(strict, drop, 0.0)
            hi = drop.astype(BF16)
            lo = (drop - hi.astype(F32)).astype(BF16)
            hl_scr[p, :, 0:n] = hi
            hl_scr[p, :, n:2 * n] = lo
            sums.append(jnp.dot(hl_scr[p], u_ref[...],
                                preferred_element_type=F32))
        cmin = None
        pvs = []
        for p in prs:
            r = sums[p]
            after = r[:, :n] if diag else r[:, :n] + c_scr[p]
            w = jnp.exp(log_beta[p] - after)
            if diag:
                w = jnp.where(strict, w, 0.0)
            pvs.append(_dot(w, vj[p]))
            c = r[:, n:] if diag else c_scr[p] + r[:, n:]
            c_scr[p] = c
            cmin = c if cmin is None else jnp.minimum(cmin, c)
        for p in prs:
            o_scr[p] = pvs[p] if diag else o_scr[p] + pvs[p]
        return jnp.min(cmin)

    cmin0 = visit(i, True)

    def cond(carry):
        j, cmin = carry
        return jnp.logical_and(j >= 0, cmin < -SB_SKIP_LOG)

    def body(carry):
        j, _ = carry
        return j - 1, visit(j, False)

    lax.while_loop(cond, body, (i - 1, cmin0))
    for p in range(pairs):
        o = jnp.where(head_a, o_scr[p, 0:n, :], o_scr[p, n:2 * n, :])
        cols = slice(p * n, (p + 1) * n)
        o_ref[:, cols] = (o * _silu(z_ref[:, cols])).astype(o_ref.dtype)


def _sb_attn(proj3):
    bsz, seq, _ = proj3.shape
    n = SB_BLOCK
    w = BRANCH_WIDTH
    base = 7
    u2 = _sb_constants()
    return pl.pallas_call(
        _sb_kernel,
        out_shape=jax.ShapeDtypeStruct((bsz, seq, w), BF16),
        grid=(bsz, seq // n),
        in_specs=[
            pl.BlockSpec((None, n, w), lambda b, i: (b, i, base)),
            pl.BlockSpec((None, seq, w), lambda b, i: (b, 0, base + 1)),
            pl.BlockSpec((None, seq, w), lambda b, i: (b, 0, base + 2)),
            pl.BlockSpec((None, n, w), lambda b, i: (b, i, base + 3)),
            pl.BlockSpec(u2.shape, lambda b, i: (0, 0)),
        ],
        out_specs=pl.BlockSpec((None, n, w), lambda b, i: (b, i, 0)),
        scratch_shapes=[pltpu.VMEM((w // n, 2 * n, n), BF16),
                        pltpu.VMEM((w // n, 2 * n, 2 * n), BF16),
                        pltpu.VMEM((w // n, 2 * n, n), F32),
                        pltpu.VMEM((w // n, 2 * n, n), F32)],
        compiler_params=pltpu.CompilerParams(
            dimension_semantics=("arbitrary", "arbitrary"),
            vmem_limit_bytes=_mib(40)),
        name="stick_breaking",
    )(proj3, proj3, proj3, proj3, jnp.asarray(u2, BF16))


def _merge_kernel(ya_ref, yb_ref, yc_ref, g0, g1, g2, g3, g4, g5, x_ref, gate_ref, wb_ref,
                  wo_ref, fnw_ref, o_ref, *, final):
    w = BRANCH_WIDTH
    ys = (ya_ref[...], yb_ref[...], yc_ref[...])
    gl = ((g0, g1), (g2, g3), (g4, g5))
    halves = []
    for half in range(2):
        acc = None
        for nb in range(N_BRANCH):
            br = jnp.dot(ys[nb], wb_ref[nb, :, half * w:(half + 1) * w],
                         preferred_element_type=F32)
            term = jax.nn.sigmoid(gl[nb][half][...]) * br
            acc = term if acc is None else acc + term
        halves.append(acc.astype(BF16))
    merged = jnp.concatenate(halves, axis=1)
    out = jnp.dot(merged, wo_ref[...], preferred_element_type=F32)
    xn = x_ref[...] + gate_ref[...] * out
    if final:
        ms = jnp.mean(xn * xn, axis=-1, keepdims=True)
        xn = xn * lax.rsqrt(ms + EPS) * fnw_ref[...]
    o_ref[...] = xn


def _merge(ya, yb, yc, proj, x2, mod5, wb_bf16, wo_bf16, final_norm_w, layer, seq, final):
    m, d = x2.shape
    w = BRANCH_WIDTH
    tm = 512
    per_seq = seq // tm
    gate0 = 11
    ycol = lambda: pl.BlockSpec((tm, w), lambda i: (i, 0))
    gcol = lambda cb: pl.BlockSpec((tm, w), lambda i: (i, gate0 + cb))
    return pl.pallas_call(
        functools.partial(_merge_kernel, final=final),
        out_shape=jax.ShapeDtypeStruct((m, d), F32),
        grid=(m // tm,),
        in_specs=[
            ycol(), ycol(), ycol(),
            gcol(0), gcol(1), gcol(2), gcol(3), gcol(4), gcol(5),
            pl.BlockSpec((tm, d), lambda i: (i, 0)),
            pl.BlockSpec((None, None, None, 1, d), lambda i: (layer, i // per_seq, 2, 0, 0)),
            pl.BlockSpec((None, N_BRANCH, w, d), lambda i: (layer, 0, 0, 0)),
            pl.BlockSpec((None, d, d), lambda i: (layer, 0, 0)),
            pl.BlockSpec((1, d), lambda i: (0, 0)),
        ],
        out_specs=pl.BlockSpec((tm, d), lambda i: (i, 0)),
        compiler_params=pltpu.CompilerParams(
            dimension_semantics=("arbitrary",),
            vmem_limit_bytes=_mib(48)),
        name="merge_out",
    )(ya, yb, yc, proj, proj, proj, proj, proj, proj, x2, mod5, wb_bf16, wo_bf16,
      final_norm_w.reshape(1, d))


def kernel(x, c, ada_w, ada_b, norm_w, w_in, hgrn_lb, hgrn_norm_w, conv_w, conv_b, conv_ln_w,
           conv_ln_b, w_branch, w_out, final_norm_w):
    bsz, seq, d = x.shape
    depth = ada_w.shape[0]
    m = bsz * seq
    mod = _ada_mod(c, ada_w, ada_b)
    mod5 = mod.reshape(depth, bsz, 3, 1, d)
    w_in_b = w_in.astype(BF16)
    wb_b = w_branch.astype(BF16)
    wo_b = w_out.astype(BF16)
    x2 = x.reshape(m, d)
    for layer in range(depth):
        proj = _inproj(x2, mod5, norm_w, w_in_b, layer, seq)
        proj3 = proj.reshape(bsz, seq, proj.shape[-1])
        ya = _hgrn(proj3, hgrn_lb, hgrn_norm_w, layer)
        yb =_conv(proj3, conv_w, conv_b, conv_ln_w, conv_ln_b, layer)
        yc = _sb_attn(proj3)
        x2 = _merge(ya.reshape(m, -1), yb.reshape(m, -1), yc.reshape(m, -1), proj, x2, mod5,
                    wb_b, wo_b, final_norm_w, layer, seq, final=(layer == depth - 1))
    return x2.reshape(bsz, seq, d)
```

```python
import functools

import numpy as np
import jax
import jax.numpy as jnp
from jax import lax
from jax.experimental import pallas as pl
from jax.experimental.pallas import tpu as pltpu

F32 = jnp.float32
BF16 = jnp.bfloat16

LANES_V7X = 128
SUBLANES_V7X = 8
VMEM_BYTES_V7X = 64 * 1024 * 1024

EPS = 1e-6
TINY = 1e-30
LOG2E = 1.4426950408889634
BRANCH_WIDTH = 512
HGRN_HEADS = 4
HGRN_HEAD_DIM = BRANCH_WIDTH // HGRN_HEADS
HGRN_CHUNK = 64
HGRN_LEVELS = 6
_HGRN_MXU_LEVELS = 3
CONV_WIDTH = 31
CONV_HALO = 32
SB_HEADS = 8
SB_HEAD_DIM = BRANCH_WIDTH // SB_HEADS
SB_BLOCK = 128
SB_SKIP_LOG = -104.0
N_BRANCH = 3


def _mib(n):
    return int(n) * 1024 * 1024


def _silu(x):
    return x * jax.nn.sigmoid(x)


def _dot(a, b):
    return jnp.dot(a.astype(BF16), b.astype(BF16), preferred_element_type=F32)


def _dot_nt(a, b):
    return lax.dot_general(a.astype(BF16), b.astype(BF16), (((1,), (1,)), ((), ())),
                           preferred_element_type=F32)


def _dot_tn(a, b):
    return lax.dot_general(a.astype(BF16), b.astype(BF16), (((0,), (0,)), ((), ())),
                           preferred_element_type=F32)


def _split3(x):
    hi = x.astype(BF16)
    r = x - hi.astype(F32)
    mid = r.astype(BF16)
    lo = (r - mid.astype(F32)).astype(BF16)
    return hi, mid, lo


def _ada_kernel(c_ref, w_ref, b_ref, o_ref):
    c = c_ref[...]
    ca = _silu(c)
    w = w_ref[...]
    ch = ca.astype(BF16)
    cl = (ca - ch.astype(F32)).astype(BF16)
    wh = w.astype(BF16)
    wl = (w - wh.astype(F32)).astype(BF16)
    acc = (jnp.dot(ch, wh, preferred_element_type=F32)
           + jnp.dot(ch, wl, preferred_element_type=F32)
           + jnp.dot(cl, wh, preferred_element_type=F32))
    o_ref[...] = acc + b_ref[...]


def _ada_mod(c, ada_w, ada_b):
    depth, d, n = ada_w.shape
    bsz = c.shape[0]
    tn = n // 4
    return pl.pallas_call(
        _ada_kernel,
        out_shape=jax.ShapeDtypeStruct((depth, bsz, n), F32),
        grid=(depth, n // tn),
        in_specs=[
            pl.BlockSpec((bsz, d), lambda l, j: (0, 0)),
            pl.BlockSpec((None, d, tn), lambda l, j: (l, 0, j)),
            pl.BlockSpec((None, 1, tn), lambda l, j: (l, 0, j)),
        ],
        out_specs=pl.BlockSpec((None, bsz, tn), lambda l, j: (l, 0, j)),
        compiler_params=pltpu.CompilerParams(
            dimension_semantics=("arbitrary", "arbitrary"),
            vmem_limit_bytes=_mib(32)),
        name="ada_mod",
    )(c, ada_w, ada_b.reshape(depth, 1, n))


_INPROJ_NORM_ROWS = 256
_INPROJ_MM_ROWS = 512


def _inproj_kernel(x_ref, shift_ref, scale_ref, nw_ref, w_ref, o_ref, h_scr):
    tm = x_ref.shape[0]

    @pl.when(pl.program_id(1) == 0)
    def _():
        gain = nw_ref[...] * (1.0 + scale_ref[...])
        shift = shift_ref[...]

        def body(i, carry):
            r0 = pl.multiple_of(i * _INPROJ_NORM_ROWS, _INPROJ_NORM_ROWS)
            x = x_ref[pl.ds(r0, _INPROJ_NORM_ROWS), :]
            ms = jnp.mean(x * x, axis=-1, keepdims=True)
            h = (x * lax.rsqrt(ms + EPS)) * gain + shift
            h_scr[pl.ds(r0, _INPROJ_NORM_ROWS), :] = h.astype(BF16)
            return carry

        lax.fori_loop(0, tm // _INPROJ_NORM_ROWS, body, 0)

    def mm(i, carry):
        r0 = pl.multiple_of(i * _INPROJ_MM_ROWS, _INPROJ_MM_ROWS)
        o_ref[pl.ds(r0, _INPROJ_MM_ROWS), :] = jnp.dot(
            h_scr[pl.ds(r0, _INPROJ_MM_ROWS), :], w_ref[...],
            preferred_element_type=F32).astype(o_ref.dtype)
        return carry

    lax.fori_loop(0, tm // _INPROJ_MM_ROWS, mm, 0)


def _inproj(x2, mod5, norm_w, w_in_bf16, layer, seq):
    m, d = x2.shape
    n = w_in_bf16.shape[-1]
    tm = seq
    tn = n // 4
    per_seq = seq // tm
    return pl.pallas_call(
        _inproj_kernel,
        out_shape=jax.ShapeDtypeStruct((m, n), BF16),
        grid=(m // tm, n // tn),
        in_specs=[
            pl.BlockSpec((tm, d), lambda i, j: (i, 0)),
            pl.BlockSpec((None, None, None, 1, d), lambda i, j: (layer, i // per_seq, 0, 0, 0)),
            pl.BlockSpec((None, None, None, 1, d), lambda i, j: (layer, i // per_seq, 1, 0, 0)),
            pl.BlockSpec((None, 1, d), lambda i, j: (layer, 0, 0)),
            pl.BlockSpec((None, d, tn), lambda i, j: (layer, 0, j)),
        ],
        out_specs=pl.BlockSpec((tm, tn), lambda i, j: (i, j)),
        scratch_shapes=[pltpu.VMEM((tm, d), BF16)],
        compiler_params=pltpu.CompilerParams(
            dimension_semantics=("arbitrary", "arbitrary"),
            vmem_limit_bytes=_mib(56)),
        name="inproj",
    )(x2, mod5, mod5, norm_w.reshape(norm_w.shape[0], 1, d), w_in_bf16)


def _hgrn_constants():
    L = HGRN_CHUNK
    tri = np.tril(np.ones((L, L), np.float32))
    blocks = [tri]
    r = np.arange(L)
    for l in range(_HGRN_MXU_LEVELS):
        h = 1 << l
        p = (r // (2 * h)) * (2 * h) + h - 1
        blocks.append(tri - tri[p])
    c_all = np.concatenate(blocks, axis=0)
    c_all = np.concatenate([c_all, c_all, c_all], axis=1)
    t = r[:, None]
    s = r[None, :]
    x = t ^ s
    lev = np.where(x > 0, np.floor(np.log2(np.maximum(x, 1))).astype(np.int32), HGRN_LEVELS)
    lev = np.where(s > t, -1, lev).astype(np.int32)
    return c_all, lev


def _hgrn_kernel(q_ref, f_ref, i_ref, z_ref, lb_ref, nw_ref, c_ref, lev_ref, o_ref, st_ref,
                 *, layer):
    L = HGRN_CHUNK
    K = HGRN_HEAD_DIM
    n_chunks = q_ref.shape[0] // L

    @pl.when(pl.program_id(1) == 0)
    def _():
        st_ref[...] = jnp.zeros_like(st_ref)

    lb_all = lb_ref[...]
    lb_exp = jnp.exp(lb_all - jnp.max(lb_all, axis=0, keepdims=True))
    lb_soft = lb_exp / jnp.sum(lb_exp, axis=0, keepdims=True)
    lower = jnp.zeros((1, lb_all.shape[1]), F32)
    for l in range(1, layer + 1):
        lower = lower + lb_soft[l:l + 1, :]
    one_m_lower = 1.0 - lower
    nw = nw_ref[...]

    def body(c, carry):
        r0 = pl.multiple_of(c * L, L)
        rows = pl.ds(r0, L)
        q = _silu(q_ref[rows, :].astype(F32)) * (K ** -0.5)
        fr = f_ref[rows, :].astype(F32)
        f = lower + one_m_lower * jax.nn.sigmoid(fr)
        g = jnp.log(jnp.maximum(f, TINY)) * LOG2E
        k = one_m_lower * jax.nn.sigmoid(-fr)
        v = i_ref[rows, :]
        z = z_ref[rows, :].astype(F32)
        dall = jnp.dot(c_ref[...], jnp.concatenate(_split3(g), axis=0),
                       preferred_element_type=F32)
        lev = lev_ref[...]
        heads = range(HGRN_HEADS)
        hcols = [slice(h * K, (h + 1) * K) for h in heads]

        def boundary_diff(b, l, cols):
            if l < _HGRN_MXU_LEVELS:
                return dall[(l + 1) * L:(l + 2) * L, cols]
            hw = 1 << l
            parts = [jnp.broadcast_to(b[s + hw - 1:s + hw, :], (2 * hw, K))
                     for s in range(0, L, 2 * hw)]
            return b - (parts[0] if len(parts) == 1 else jnp.concatenate(parts, axis=0))

        pair, o_inter, upd, decay = [], [], [], []
        for h in heads:
            cols = hcols[h]
            qh, kh, vh = q[:, cols], k[:, cols], v[:, cols]
            b = dall[0:L, cols]
            b_last = b[L - 1:L, :]
            o_inter.append(_dot_nt(qh * jnp.exp2(b), st_ref[h]))
            upd.append(_dot_tn(vh, kh * jnp.exp2(b_last - b)))
            decay.append(jnp.exp2(b_last))
            prods = [_dot_nt(qh, kh)]
            for l in range(HGRN_LEVELS):
                fac = jnp.exp2(-jnp.abs(boundary_diff(b, l, cols)))
                prods.append(_dot_nt(qh * fac, kh * fac))
            pair.append(prods)
        outs = []
        for h in heads:
            a = jnp.where(lev == HGRN_LEVELS, pair[h][0], 0.0)
            for l in range(HGRN_LEVELS):
                a = jnp.where(lev == l, pair[h][l + 1], a)
            outs.append(o_inter[h] + _dot(a, v[:, hcols[h]]))
        for h in heads:
            st_ref[h] = st_ref[h] * decay[h] + upd[h]
            o = outs[h]
            on = o * lax.rsqrt(jnp.mean(o * o, axis=-1, keepdims=True) + EPS) * nw
            o_ref[rows, hcols[h]] = (on * _silu(z[:, hcols[h]])).astype(o_ref.dtype)
        return carry

    lax.fori_loop(0, n_chunks, body, 0, unroll=2)


def _hgrn(proj3, hgrn_lb, hgrn_norm_w, layer):
    bsz, seq, _ = proj3.shape
    w = BRANCH_WIDTH
    tb = 512
    c_all, lev = _hgrn_constants()
    col = lambda cb: pl.BlockSpec((None, tb, w), lambda b, s: (b, s, cb))
    return pl.pallas_call(
        functools.partial(_hgrn_kernel, layer=layer),
        out_shape=jax.ShapeDtypeStruct((bsz, seq, w), BF16),
        grid=(bsz, seq // tb),
        in_specs=[
            col(0), col(1), col(2), col(3),
            pl.BlockSpec(hgrn_lb.shape, lambda b, s: (0, 0)),
            pl.BlockSpec((None, 1, HGRN_HEAD_DIM), lambda b, s: (layer, 0, 0)),
            pl.BlockSpec(c_all.shape, lambda b, s: (0, 0)),
            pl.BlockSpec(lev.shape, lambda b, s: (0, 0)),
        ],
        out_specs=pl.BlockSpec((None, tb, w), lambda b, s: (b, s, 0)),
        scratch_shapes=[pltpu.VMEM((HGRN_HEADS, HGRN_HEAD_DIM, HGRN_HEAD_DIM), F32)],
        compiler_params=pltpu.CompilerParams(
            dimension_semantics=("arbitrary", "arbitrary"),
            vmem_limit_bytes=_mib(32)),
        name="hgrn2",
    )(proj3, proj3, proj3, proj3, hgrn_lb,
      hgrn_norm_w.reshape(hgrn_norm_w.shape[0], 1, HGRN_HEAD_DIM),
      jnp.asarray(c_all, BF16), jnp.asarray(lev))


_CONV_ROWS = 32


def _conv_kernel(a_ref, g_ref, z_ref, w_ref, b_ref, lnw_ref, lnb_ref, o_ref, u_scr):
    ts = a_ref.shape[0]

    @pl.when(pl.program_id(1) == 0)
    def _():
        u_scr[0:CONV_HALO, :] = jnp.zeros((CONV_HALO, u_scr.shape[1]), F32)

    @pl.when(pl.program_id(1) > 0)
    def _():
        u_scr[0:CONV_HALO, :] = u_scr[ts:ts + CONV_HALO, :]

    u_scr[CONV_HALO:CONV_HALO + ts, :] = (a_ref[...].astype(F32)
                                          * jax.nn.sigmoid(g_ref[...].astype(F32)))

    bias = b_ref[...]
    lnw = lnw_ref[...]
    lnb = lnb_ref[...]
    first = CONV_HALO - (CONV_WIDTH - 1)
    sub = SUBLANES_V7X
    for c in range(ts // _CONV_ROWS):
        r0 = c * _CONV_ROWS
        acc = jnp.broadcast_to(bias, (_CONV_ROWS, bias.shape[1]))
        for res in range(sub):
            taps = [j for j in range(CONV_WIDTH) if (first + j) % sub == res]
            rows = _CONV_ROWS + (sub if res else 0)
            part = None
            for j in taps:
                a0 = r0 + first + j - res
                term = w_ref[j:j + 1, :] * u_scr[a0:a0 + rows, :]
                part = term if part is None else part + term
            acc = acc + part[res:res + _CONV_ROWS, :]
        mu = jnp.mean(acc, axis=-1, keepdims=True)
        d = acc - mu
        var = jnp.mean(d * d, axis=-1, keepdims=True)
        y = _silu(d * lax.rsqrt(var + EPS) * lnw + lnb)
        z = z_ref[r0:r0 + _CONV_ROWS, :].astype(F32)
        o_ref[r0:r0 + _CONV_ROWS, :] = (y * _silu(z)).astype(o_ref.dtype)


def _conv(proj3, conv_w, conv_b, ln_w, ln_b, layer):
    bsz, seq, _ = proj3.shape
    w = BRANCH_WIDTH
    ts = 256
    col = lambda cb: pl.BlockSpec((None, ts, w), lambda b, s: (b, s, cb))
    vec = lambda: pl.BlockSpec((None, 1, w), lambda b, s: (layer, 0, 0))
    depth = conv_w.shape[0]
    return pl.pallas_call(
        _conv_kernel,
        out_shape=jax.ShapeDtypeStruct((bsz, seq, w), BF16),
        grid=(bsz, seq // ts),
        in_specs=[
            col(4), col(5), col(6),
            pl.BlockSpec((None, CONV_WIDTH, w), lambda b, s: (layer, 0, 0)),
            vec(), vec(), vec(),
        ],
        out_specs=pl.BlockSpec((None, ts, w), lambda b, s: (b, s, 0)),
        scratch_shapes=[pltpu.VMEM((CONV_HALO + ts, w), F32)],
        compiler_params=pltpu.CompilerParams(
            dimension_semantics=("arbitrary", "arbitrary"),
            vmem_limit_bytes=_mib(32)),
        name="conv_module",
    )(proj3, proj3, proj3, conv_w, conv_b.reshape(depth, 1, w), ln_w.reshape(depth, 1, w),
      ln_b.reshape(depth, 1, w))


def _sb_constants():
    n = SB_BLOCK
    u = (np.arange(n)[:, None] > np.arange(n)[None, :]).astype(np.float32)
    u1 = np.concatenate([u, np.ones((n, n), np.float32)], axis=1)
    return np.concatenate([u1, u1], axis=0)


def _sb_kernel(q_ref, k_ref, v_ref, z_ref, u_ref, o_ref, q_scr, hl_scr, c_scr, o_scr, cmin_ref):
    n = SB_BLOCK
    pairs = q_ref.shape[1] // n
    prs = range(pairs)
    i = pl.program_id(1)
    lane = lax.broadcasted_iota(jnp.int32, (2 * n, n), 1)
    row = lax.broadcasted_iota(jnp.int32, (2 * n, n), 0)
    strict = lane < (row & (n - 1))
    head_a = lax.broadcasted_iota(jnp.int32, (n, n), 1) < SB_HEAD_DIM

    for p in prs:
        q = q_ref[:, p * n:(p + 1) * n].astype(F32) * (SB_HEAD_DIM ** -0.5)
        q_scr[p, 0:n, :] = jnp.where(head_a, q, 0.0).astype(BF16)
        q_scr[p, n:2 * n, :] = jnp.where(head_a, 0.0, q).astype(BF16)

    def visit(blocks, fresh):
        units = [(p, t) for p in prs for t in range(len(blocks))]
        kv = {}
        for t, (j, _) in enumerate(blocks):
            rows = pl.ds(pl.multiple_of(j * n, n), n)
            for p in prs:
                kv[p, t] = (k_ref[rows, p * n:(p + 1) * n], v_ref[rows, p * n:(p + 1) * n])
        logits = {u: _dot_nt(q_scr[u[0]], kv[u][0]) for u in units}
        log_beta, sums = {}, {}
        for u in units:
            p, t = u
            x = logits[u]
            lp = jnp.log(1.0 + jnp.exp2(jnp.abs(x) * (-LOG2E)))
            log_beta[u] = jnp.minimum(x, 0.0) - lp
            drop = jnp.maximum(x, 0.0) + lp
            if blocks[t][1]:
                drop = jnp.where(strict, drop, 0.0)
            hi = drop.astype(BF16)
            lo = (drop - hi.astype(F32)).astype(BF16)
            hl_scr[p, t, :, 0:n] = hi
            hl_scr[p, t, :, n:2 * n] = lo
            sums[u] = jnp.dot(hl_scr[p, t], u_ref[...],
                              preferred_element_type=F32)
        cmin = None
        pvs = []
        for p in prs:
            c = None if fresh else c_scr[p]
            pv = None
            for t in range(len(blocks)):
                r = sums[p, t]
                after = r[:, :n] if c is None else r[:, :n] + c
                w = jnp.exp(log_beta[p, t] - after)
                if blocks[t][1]:
                    w = jnp.where(strict, w, 0.0)
                term = _dot(w, kv[p, t][1])
                pv = term if pv is None else pv + term
                c = r[:, n:] if c is None else c + r[:, n:]
            c_scr[p] = c
            pvs.append(pv)
            cmin = c if cmin is None else jnp.minimum(cmin, c)
        for p in prs:
            o_scr[p] = pvs[p] if fresh else o_scr[p] + pvs[p]
        return jnp.min(cmin)

    @pl.when(i == 0)
    def _():
        cmin_ref[0] = visit([(i, True)], True)

    @pl.when(i > 0)
    def _():
        cmin_ref[0] = visit([(i, True), (i - 1, False)], True)

    def cond(carry):
        j, cmin = carry
        return jnp.logical_and(j >= 0, cmin < -SB_SKIP_LOG)

    def body(carry):
        j, _ = carry
        return j - 1, visit([(j, False)], False)

    lax.while_loop(cond, body, (i - 2, cmin_ref[0]))
    for p in range(pairs):
        o = jnp.where(head_a, o_scr[p, 0:n, :], o_scr[p, n:2 * n, :])
        cols = slice(p * n, (p + 1) * n)
        o_ref[:, cols] = (o * _silu(z_ref[:, cols].astype(F32))).astype(o_ref.dtype)


def _sb_attn(proj3):
    bsz, seq, _ = proj3.shape
    n = SB_BLOCK
    w = BRANCH_WIDTH
    base = 7
    u2 = _sb_constants()
    return pl.pallas_call(
        _sb_kernel,
        out_shape=jax.ShapeDtypeStruct((bsz, seq, w), BF16),
        grid=(bsz, seq // n),
        in_specs=[
            pl.BlockSpec((None, n, w), lambda b, i: (b, i, base)),
            pl.BlockSpec((None, seq, w), lambda b, i: (b, 0, base + 1)),
            pl.BlockSpec((None, seq, w), lambda b, i: (b, 0, base + 2)),
            pl.BlockSpec((None, n, w), lambda b, i: (b, i, base + 3)),
            pl.BlockSpec(u2.shape, lambda b, i: (0, 0)),
        ],
        out_specs=pl.BlockSpec((None, n, w), lambda b, i: (b, i, 0)),
        scratch_shapes=[pltpu.VMEM((w // n, 2 * n, n), BF16),
                        pltpu.VMEM((w // n, 2, 2 * n, 2 * n), BF16),
                        pltpu.VMEM((w // n, 2 * n, n), F32),
                        pltpu.VMEM((w // n, 2 * n, n), F32),
                        pltpu.SMEM((1,), F32)],
        compiler_params=pltpu.CompilerParams(
            dimension_semantics=("arbitrary", "arbitrary"),
            vmem_limit_bytes=_mib(40)),
        name="stick_breaking",
    )(proj3, proj3, proj3, proj3, jnp.asarray(u2, BF16))


def _merge_kernel(ya_ref, yb_ref, yc_ref, g0, g1, g2, g3, g4, g5, x_ref, gate_ref, wb_ref,
                  wo_ref, fnw_ref, o_ref, *, final):
    w = BRANCH_WIDTH
    ys = (ya_ref[...], yb_ref[...], yc_ref[...])
    gl = ((g0, g1), (g2, g3), (g4, g5))
    halves = []
    for half in range(2):
        acc = None
        for nb in range(N_BRANCH):
            br = jnp.dot(ys[nb], wb_ref[nb, :, half * w:(half + 1) * w],
                         preferred_element_type=F32)
            term = jax.nn.sigmoid(gl[nb][half][...].astype(F32)) * br
            acc = term if acc is None else acc + term
        halves.append(acc.astype(BF16))
    merged = jnp.concatenate(halves, axis=1)
    out = jnp.dot(merged, wo_ref[...], preferred_element_type=F32)
    xn = x_ref[...] + gate_ref[...] * out
    if final:
        ms = jnp.mean(xn * xn, axis=-1, keepdims=True)
        xn = xn * lax.rsqrt(ms + EPS) * fnw_ref[...]
    o_ref[...] = xn


def _merge(ya, yb, yc, proj, x2, mod5, wb_bf16, wo_bf16, final_norm_w, layer, seq, final):
    m, d = x2.shape
    w = BRANCH_WIDTH
    tm = 512
    per_seq = seq // tm
    gate0 = 11
    ycol = lambda: pl.BlockSpec((tm, w), lambda i: (i, 0))
    gcol = lambda cb: pl.BlockSpec((tm, w), lambda i: (i, gate0 + cb))
    return pl.pallas_call(
        functools.partial(_merge_kernel, final=final),
        out_shape=jax.ShapeDtypeStruct((m, d), F32),
        grid=(m // tm,),
        in_specs=[
            ycol(), ycol(), ycol(),
            gcol(0), gcol(1), gcol(2), gcol(3), gcol(4), gcol(5),
            pl.BlockSpec((tm, d), lambda i: (i, 0)),
            pl.BlockSpec((None, None, None, 1, d), lambda i: (layer, i // per_seq, 2, 0, 0)),
            pl.BlockSpec((None, N_BRANCH, w, d), lambda i: (layer, 0, 0, 0)),
            pl.BlockSpec((None, d, d), lambda i: (layer, 0, 0)),
            pl.BlockSpec((1, d), lambda i: (0, 0)),
        ],
        out_specs=pl.BlockSpec((tm, d), lambda i: (i, 0)),
        compiler_params=pltpu.CompilerParams(
            dimension_semantics=("arbitrary",),
            vmem_limit_bytes=_mib(48)),
        name="merge_out",
    )(ya, yb, yc, proj, proj, proj, proj, proj, proj, x2, mod5, wb_bf16, wo_bf16,
      final_norm_w.reshape(1, d))


def kernel(x, c, ada_w, ada_b, norm_w, w_in, hgrn_lb, hgrn_norm_w, conv_w, conv_b, conv_ln_w,
           conv_ln_b, w_branch, w_out, final_norm_w):
    bsz, seq, d = x.shape
    depth = ada_w.shape[0]
    m = bsz * seq
    mod = _ada_mod(c, ada_w, ada_b)
    mod5 = mod.reshape(depth, bsz, 3, 1, d)
    w_in_b = w_in.astype(BF16)
    wb_b = w_branch.astype(BF16)
    wo_b = w_out.astype(BF16)
    x2 = x.reshape(m, d)
    for layer in range(depth):
        proj = _inproj(x2, mod5, norm_w, w_in_b, layer, seq)
        proj3 = proj.reshape(bsz, seq, proj.shape[-1])
        ya = _hgrn(proj3, hgrn_lb, hgrn_norm_w, layer)
        yb =_conv(proj3, conv_w, conv_b, conv_ln_w, conv_ln_b, layer)
        yc = _sb_attn(proj3)
        x2 = _merge(ya.reshape(m, -1), yb.reshape(m, -1), yc.reshape(m, -1), proj, x2, mod5,
                    wb_b, wo_b, final_norm_w, layer, seq, final=(layer == depth - 1))
    return x2.reshape(bsz, seq, d)
```

```python
import functools

import numpy as np
import jax
import jax.numpy as jnp
from jax import lax
from jax.experimental import pallas as pl
from jax.experimental.pallas import tpu as pltpu

F32 = jnp.float32
BF16 = jnp.bfloat16

LANES_V7X = 128
SUBLANES_V7X = 8
VMEM_BYTES_V7X = 64 * 1024 * 1024

EPS = 1e-6
TINY = 1e-30
LOG2E = 1.4426950408889634
BRANCH_WIDTH = 512
HGRN_HEADS = 4
HGRN_HEAD_DIM = BRANCH_WIDTH // HGRN_HEADS
HGRN_CHUNK = 64
HGRN_LEVELS = 6
_HGRN_MXU_LEVELS = 3
CONV_WIDTH = 31
CONV_HALO = 32
SB_HEADS = 8
SB_HEAD_DIM = BRANCH_WIDTH // SB_HEADS
SB_BLOCK = 128
SB_TOP_ROWS = 32
SB_SKIP_LOG = -104.0
N_BRANCH = 3


def _mib(n):
    return int(n) * 1024 * 1024


def _silu(x):
    return x * jax.nn.sigmoid(x)


def _dot(a, b):
    return jnp.dot(a.astype(BF16), b.astype(BF16), preferred_element_type=F32)


def _dot_nt(a, b):
    return lax.dot_general(a.astype(BF16), b.astype(BF16), (((1,), (1,)), ((), ())),
                           preferred_element_type=F32)


def _dot_tn(a, b):
    return lax.dot_general(a.astype(BF16), b.astype(BF16), (((0,), (0,)), ((), ())),
                           preferred_element_type=F32)


def _split3(x):
    hi = x.astype(BF16)
    r = x - hi.astype(F32)
    mid = r.astype(BF16)
    lo = (r - mid.astype(F32)).astype(BF16)
    return hi, mid, lo


def _ada_kernel(c_ref, w_ref, b_ref, o_ref):
    c = c_ref[...]
    ca = _silu(c)
    w = w_ref[...]
    ch = ca.astype(BF16)
    cl = (ca - ch.astype(F32)).astype(BF16)
    wh = w.astype(BF16)
    wl = (w - wh.astype(F32)).astype(BF16)
    acc = (jnp.dot(ch, wh, preferred_element_type=F32)
           + jnp.dot(ch, wl, preferred_element_type=F32)
           + jnp.dot(cl, wh, preferred_element_type=F32))
    o_ref[...] = acc + b_ref[...]


def _ada_mod(c, ada_w, ada_b):
    depth, d, n = ada_w.shape
    bsz = c.shape[0]
    tn = n // 4
    return pl.pallas_call(
        _ada_kernel,
        out_shape=jax.ShapeDtypeStruct((depth, bsz, n), F32),
        grid=(depth, n // tn),
        in_specs=[
            pl.BlockSpec((bsz, d), lambda l, j: (0, 0)),
            pl.BlockSpec((None, d, tn), lambda l, j: (l, 0, j)),
            pl.BlockSpec((None, 1, tn), lambda l, j: (l, 0, j)),
        ],
        out_specs=pl.BlockSpec((None, bsz, tn), lambda l, j: (l, 0, j)),
        compiler_params=pltpu.CompilerParams(
            dimension_semantics=("arbitrary", "arbitrary"),
            vmem_limit_bytes=_mib(32)),
        name="ada_mod",
    )(c, ada_w, ada_b.reshape(depth, 1, n))


_INPROJ_NORM_ROWS = 256
_INPROJ_MM_ROWS = 512


def _inproj_kernel(x_ref, shift_ref, scale_ref, nw_ref, w_ref, o_ref, h_scr):
    tm = x_ref.shape[0]

    @pl.when(pl.program_id(1) == 0)
    def _():
        gain = nw_ref[...] * (1.0 + scale_ref[...])
        shift = shift_ref[...]

        def body(i, carry):
            r0 = pl.multiple_of(i * _INPROJ_NORM_ROWS, _INPROJ_NORM_ROWS)
            x = x_ref[pl.ds(r0, _INPROJ_NORM_ROWS), :]
            ms = jnp.mean(x * x, axis=-1, keepdims=True)
            h = (x * lax.rsqrt(ms + EPS)) * gain + shift
            h_scr[pl.ds(r0, _INPROJ_NORM_ROWS), :] = h.astype(BF16)
            return carry

        lax.fori_loop(0, tm // _INPROJ_NORM_ROWS, body, 0)

    def mm(i, carry):
        r0 = pl.multiple_of(i * _INPROJ_MM_ROWS, _INPROJ_MM_ROWS)
        o_ref[pl.ds(r0, _INPROJ_MM_ROWS), :] = jnp.dot(
            h_scr[pl.ds(r0, _INPROJ_MM_ROWS), :], w_ref[...],
            preferred_element_type=F32).astype(o_ref.dtype)
        return carry

    lax.fori_loop(0, tm // _INPROJ_MM_ROWS, mm, 0)


def _inproj(x2, mod5, norm_w, w_in_bf16, layer, seq):
    m, d = x2.shape
    n = w_in_bf16.shape[-1]
    tm = seq
    tn = n // 4
    per_seq = seq // tm
    return pl.pallas_call(
        _inproj_kernel,
        out_shape=jax.ShapeDtypeStruct((m, n), BF16),
        grid=(m // tm, n // tn),
        in_specs=[
            pl.BlockSpec((tm, d), lambda i, j: (i, 0)),
            pl.BlockSpec((None, None, None, 1, d), lambda i, j: (layer, i // per_seq, 0, 0, 0)),
            pl.BlockSpec((None, None, None, 1, d), lambda i, j: (layer, i // per_seq, 1, 0, 0)),
            pl.BlockSpec((None, 1, d), lambda i, j: (layer, 0, 0)),
            pl.BlockSpec((None, d, tn), lambda i, j: (layer, 0, j)),
        ],
        out_specs=pl.BlockSpec((tm, tn), lambda i, j: (i, j)),
        scratch_shapes=[pltpu.VMEM((tm, d), BF16)],
        compiler_params=pltpu.CompilerParams(
            dimension_semantics=("arbitrary", "arbitrary"),
            vmem_limit_bytes=_mib(56)),
        name="inproj",
    )(x2, mod5, mod5, norm_w.reshape(norm_w.shape[0], 1, d), w_in_bf16)


def _hgrn_constants():
    L = HGRN_CHUNK
    tri = np.tril(np.ones((L, L), np.float32))
    blocks = [tri]
    r = np.arange(L)
    for l in range(_HGRN_MXU_LEVELS):
        h = 1 << l
        p = (r // (2 * h)) * (2 * h) + h - 1
        blocks.append(tri - tri[p])
    c_all = np.concatenate(blocks, axis=0)
    c_all = np.concatenate([c_all, c_all, c_all], axis=1)
    t = r[:, None]
    s = r[None, :]
    x = t ^ s
    lev = np.where(x > 0, np.floor(np.log2(np.maximum(x, 1))).astype(np.int32), HGRN_LEVELS)
    lev = np.where(s > t, -1, lev).astype(np.int32)
    return c_all, lev


def _hgrn_kernel(q_ref, f_ref, i_ref, z_ref, lb_ref, nw_ref, c_ref, lev_ref, o_ref, st_ref,
                 *, layer):
    L = HGRN_CHUNK
    K = HGRN_HEAD_DIM
    n_chunks = q_ref.shape[0] // L

    @pl.when(pl.program_id(1) == 0)
    def _():
        st_ref[...] = jnp.zeros_like(st_ref)

    lb_all = lb_ref[...]
    lb_exp = jnp.exp(lb_all - jnp.max(lb_all, axis=0, keepdims=True))
    lb_soft = lb_exp / jnp.sum(lb_exp, axis=0, keepdims=True)
    lower = jnp.zeros((1, lb_all.shape[1]), F32)
    for l in range(1, layer + 1):
        lower = lower + lb_soft[l:l + 1, :]
    one_m_lower = 1.0 - lower
    nw = nw_ref[...]

    def body(c, carry):
        r0 = pl.multiple_of(c * L, L)
        rows = pl.ds(r0, L)
        q = _silu(q_ref[rows, :].astype(F32)) * (K ** -0.5)
        fr = f_ref[rows, :].astype(F32)
        f = lower + one_m_lower * jax.nn.sigmoid(fr)
        g = jnp.log(jnp.maximum(f, TINY)) * LOG2E
        k = one_m_lower * jax.nn.sigmoid(-fr)
        v = i_ref[rows, :]
        z = z_ref[rows, :].astype(F32)
        dall = jnp.dot(c_ref[...], jnp.concatenate(_split3(g), axis=0),
                       preferred_element_type=F32)
        lev = lev_ref[...]
        heads = range(HGRN_HEADS)
        hcols = [slice(h * K, (h + 1) * K) for h in heads]

        def boundary_diff(b, l, cols):
            if l < _HGRN_MXU_LEVELS:
                return dall[(l + 1) * L:(l + 2) * L, cols]
            hw = 1 << l
            parts = [jnp.broadcast_to(b[s + hw - 1:s + hw, :], (2 * hw, K))
                     for s in range(0, L, 2 * hw)]
            return b - (parts[0] if len(parts) == 1 else jnp.concatenate(parts, axis=0))

        pair, o_inter, upd, decay = [], [], [], []
        for h in heads:
            cols = hcols[h]
            qh, kh, vh = q[:, cols], k[:, cols], v[:, cols]
            b = dall[0:L, cols]
            b_last = b[L - 1:L, :]
            o_inter.append(_dot_nt(qh * jnp.exp2(b), st_ref[h]))
            upd.append(_dot_tn(vh, kh * jnp.exp2(b_last - b)))
            decay.append(jnp.exp2(b_last))
            prods = [_dot_nt(qh, kh)]
            for l in range(HGRN_LEVELS):
                fac = jnp.exp2(-jnp.abs(boundary_diff(b, l, cols)))
                prods.append(_dot_nt(qh * fac, kh * fac))
            pair.append(prods)
        outs = []
        for h in heads:
            a = jnp.where(lev == HGRN_LEVELS, pair[h][0], 0.0)
            for l in range(HGRN_LEVELS):
                a = jnp.where(lev == l, pair[h][l + 1], a)
            outs.append(o_inter[h] + _dot(a, v[:, hcols[h]]))
        for h in heads:
            st_ref[h] = st_ref[h] * decay[h] + upd[h]
            o = outs[h]
            on = o * lax.rsqrt(jnp.mean(o * o, axis=-1, keepdims=True) + EPS) * nw
            o_ref[rows, hcols[h]] = (on * _silu(z[:, hcols[h]])).astype(o_ref.dtype)
        return carry

    lax.fori_loop(0, n_chunks, body, 0, unroll=4)


def _hgrn(proj3, hgrn_lb, hgrn_norm_w, layer):
    bsz, seq, _ = proj3.shape
    w = BRANCH_WIDTH
    tb = 512
    c_all, lev = _hgrn_constants()
    col = lambda cb: pl.BlockSpec((None, tb, w), lambda b, s: (b, s, cb))
    return pl.pallas_call(
        functools.partial(_hgrn_kernel, layer=layer),
        out_shape=jax.ShapeDtypeStruct((bsz, seq, w), BF16),
        grid=(bsz, seq // tb),
        in_specs=[
            col(0), col(1), col(2), col(3),
            pl.BlockSpec(hgrn_lb.shape, lambda b, s: (0, 0)),
            pl.BlockSpec((None, 1, HGRN_HEAD_DIM), lambda b, s: (layer, 0, 0)),
            pl.BlockSpec(c_all.shape, lambda b, s: (0, 0)),
            pl.BlockSpec(lev.shape, lambda b, s: (0, 0)),
        ],
        out_specs=pl.BlockSpec((None, tb, w), lambda b, s: (b, s, 0)),
        scratch_shapes=[pltpu.VMEM((HGRN_HEADS, HGRN_HEAD_DIM, HGRN_HEAD_DIM), F32)],
        compiler_params=pltpu.CompilerParams(
            dimension_semantics=("arbitrary", "arbitrary"),
            vmem_limit_bytes=_mib(32)),
        name="hgrn2",
    )(proj3, proj3, proj3, proj3, hgrn_lb,
      hgrn_norm_w.reshape(hgrn_norm_w.shape[0], 1, HGRN_HEAD_DIM),
      jnp.asarray(c_all, BF16), jnp.asarray(lev))


_CONV_ROWS = 32
_CONV_GLU_ROWS = 64


def _conv_kernel(a_ref, g_ref, z_ref, w_ref, b_ref, lnw_ref, lnb_ref, o_ref, u_scr, wb_scr):
    ts = a_ref.shape[0]
    sub = SUBLANES_V7X

    @pl.when(pl.program_id(1) == 0)
    def _():
        u_scr[:, 0:CONV_HALO, :] = jnp.zeros((sub, CONV_HALO, u_scr.shape[2]), F32)

    @pl.when(pl.program_id(1) > 0)
    def _():
        for res in range(sub):
            u_scr[res, 0:CONV_HALO - res, :] = u_scr[res, ts:ts + CONV_HALO - res, :]

    for c in range(ts // _CONV_GLU_ROWS):
        r0 = c * _CONV_GLU_ROWS
        u = (a_ref[r0:r0 + _CONV_GLU_ROWS, :].astype(F32)
             * jax.nn.sigmoid(g_ref[r0:r0 + _CONV_GLU_ROWS, :].astype(F32)))
        for res in range(sub):
            u_scr[res, CONV_HALO - res + r0:CONV_HALO - res + r0 + _CONV_GLU_ROWS, :] = u

    wd = w_ref.shape[1]
    for j in range(CONV_WIDTH):
        wb_scr[j] = jnp.broadcast_to(w_ref[j:j + 1, :], (sub, wd))
    wb_scr[CONV_WIDTH] = jnp.broadcast_to(b_ref[...], (sub, wd))
    lnw = lnw_ref[...]
    lnb = lnb_ref[...]
    first = CONV_HALO - (CONV_WIDTH - 1)
    tiles = _CONV_ROWS // sub

    def chunk(c, carry):
        r0 = pl.multiple_of(c * _CONV_ROWS, _CONV_ROWS)
        acc = jnp.broadcast_to(wb_scr[CONV_WIDTH][None], (tiles, sub, wd))
        for j in range(CONV_WIDTH):
            res = (first + j) % sub
            win = u_scr[res, pl.ds(r0 + (first + j - res), _CONV_ROWS), :]
            acc = acc + wb_scr[j][None] * win.reshape(tiles, sub, wd)
        acc = acc.reshape(_CONV_ROWS, wd)
        mu = jnp.mean(acc, axis=-1, keepdims=True)
        d = acc - mu
        var = jnp.mean(d * d, axis=-1, keepdims=True)
        y = _silu(d * lax.rsqrt(var + EPS) * lnw + lnb)
        z = z_ref[pl.ds(r0, _CONV_ROWS), :].astype(F32)
        o_ref[pl.ds(r0, _CONV_ROWS), :] = (y * _silu(z)).astype(o_ref.dtype)
        return carry

    lax.fori_loop(0, ts // _CONV_ROWS, chunk, 0, unroll=4)


def _conv(proj3, conv_w, conv_b, ln_w, ln_b, layer):
    bsz, seq, _ = proj3.shape
    w = BRANCH_WIDTH
    ts = 256
    col = lambda cb: pl.BlockSpec((None, ts, w), lambda b, s: (b, s, cb))
    vec = lambda: pl.BlockSpec((None, 1, w), lambda b, s: (layer, 0, 0))
    depth = conv_w.shape[0]
    return pl.pallas_call(
        _conv_kernel,
        out_shape=jax.ShapeDtypeStruct((bsz, seq, w), BF16),
        grid=(bsz, seq // ts),
        in_specs=[
            col(4), col(5), col(6),
            pl.BlockSpec((None, CONV_WIDTH, w), lambda b, s: (layer, 0, 0)),
            vec(), vec(), vec(),
        ],
        out_specs=pl.BlockSpec((None, ts, w), lambda b, s: (b, s, 0)),
        scratch_shapes=[pltpu.VMEM((SUBLANES_V7X, CONV_HALO + ts, w), F32),
                        pltpu.VMEM((CONV_WIDTH + 1, SUBLANES_V7X, w), F32)],
        compiler_params=pltpu.CompilerParams(
            dimension_semantics=("arbitrary", "arbitrary"),
            vmem_limit_bytes=_mib(32)),
        name="conv_module",
    )(proj3, proj3, proj3, conv_w, conv_b.reshape(depth, 1, w), ln_w.reshape(depth, 1, w),
      ln_b.reshape(depth, 1, w))


def _sb_constants():
    n = SB_BLOCK
    u = (np.arange(n)[:, None] > np.arange(n)[None, :]).astype(np.float32)
    u1 = np.concatenate([u, np.ones((n, n), np.float32)], axis=1)
    return np.concatenate([u1, u1], axis=0)


def _sb_kernel(q_ref, k_ref, v_ref, z_ref, u_ref, o_ref, q_scr, qtop_scr, hl_scr, c_scr, o_scr,
               cmin_ref):
    n = SB_BLOCK
    pairs = q_ref.shape[1] // n
    prs = range(pairs)
    i = pl.program_id(1)
    lane = lax.broadcasted_iota(jnp.int32, (2 * n, n), 1)
    row = lax.broadcasted_iota(jnp.int32, (2 * n, n), 0)
    strict = lane < (row & (n - 1))
    head_a = lax.broadcasted_iota(jnp.int32, (n, n), 1) < SB_HEAD_DIM

    m = SB_TOP_ROWS
    for p in prs:
        q = q_ref[:, p * n:(p + 1) * n].astype(F32) * (SB_HEAD_DIM ** -0.5)
        qa = jnp.where(head_a, q, 0.0).astype(BF16)
        qb = jnp.where(head_a, 0.0, q).astype(BF16)
        q_scr[p, 0:n, :] = qa
        q_scr[p, n:2 * n, :] = qb
        qtop_scr[p, 0:m, :] = qa[0:m]
        qtop_scr[p, m:2 * m, :] = qb[0:m]

    def top(a):
        return jnp.concatenate([a[0:m], a[n:n + m]], axis=0)

    def rest(a):
        return jnp.concatenate([a[m:n], a[n + m:2 * n]], axis=0)

    def put_top(a, a_top):
        return jnp.concatenate([a_top[0:m], a[m:n], a_top[m:2 * m], a[n + m:2 * n]], axis=0)

    def visit(blocks, fresh):
        units = [(p, t) for p in prs for t in range(len(blocks))]
        kinds = [kind for _, kind in blocks]
        kv = {}
        for t, (j, _) in enumerate(blocks):
            rows = pl.ds(pl.multiple_of(j * n, n), n)
            for p in prs:
                kv[p, t] = (k_ref[rows, p * n:(p + 1) * n], v_ref[rows, p * n:(p + 1) * n])
        logits = {(p, t): _dot_nt(qtop_scr[p] if kinds[t] == "top" else q_scr[p], kv[p, t][0])
                  for p, t in units}
        log_beta, sums = {}, {}
        for u in units:
            p, t = u
            nr = 2 * m if kinds[t] == "top" else 2 * n
            x = logits[u]
            lp = jnp.log(1.0 + jnp.exp2(jnp.abs(x) * (-LOG2E)))
            log_beta[u] = jnp.minimum(x, 0.0) - lp
            drop = jnp.maximum(x, 0.0) + lp
            if kinds[t] == "diag":
                drop = jnp.where(strict, drop, 0.0)
            hi = drop.astype(BF16)
            lo = (drop - hi.astype(F32)).astype(BF16)
            hl_scr[p, t, 0:nr, 0:n] = hi
            hl_scr[p, t, 0:nr, n:2 * n] = lo
            sums[u] = jnp.dot(hl_scr[p, t, 0:nr, :], u_ref[...],
                              preferred_element_type=F32)
        cmin = cmin_rest = None
        pvs = []
        for p in prs:
            c = None if fresh else c_scr[p]
            pv = None
            for t in range(len(blocks)):
                r = sums[p, t]
                if kinds[t] == "top":
                    c_top = top(c)
                    w = jnp.exp(log_beta[p, t] - (r[:, :n] + c_top))
                    pv = put_top(pv, top(pv) + _dot(w, kv[p, t][1]))
                    c = put_top(c, c_top + r[:, n:])
                    continue
                after = r[:, :n] if c is None else r[:, :n] + c
                w = jnp.exp(log_beta[p, t] - after)
                if kinds[t] == "diag":
                    w = jnp.where(strict, w, 0.0)
                term = _dot(w, kv[p, t][1])
                pv = term if pv is None else pv + term
                c = r[:, n:] if c is None else c + r[:, n:]
            c_scr[p] = c
            pvs.append(pv)
            cmin = c if cmin is None else jnp.minimum(cmin, c)
            cmin_rest = rest(c) if cmin_rest is None else jnp.minimum(cmin_rest, rest(c))
        for p in prs:
            o_scr[p] = pvs[p] if fresh else o_scr[p] + pvs[p]
        return jnp.min(cmin), jnp.min(cmin_rest)

    done = -SB_SKIP_LOG

    @pl.when(i == 0)
    def _():
        cmin_ref[0] = visit([(i, "diag")], True)[0]
        cmin_ref[1] = done

    @pl.when(i == 1)
    def _():
        cmin_ref[0] = visit([(i, "diag"), (i - 1, "full")], True)[0]
        cmin_ref[1] = done

    @pl.when(i >= 2)
    def _():
        cmin_ref[0], cmin_ref[1] = visit([(i, "diag"), (i - 1, "full"), (i - 2, "top")], True)

    redo = cmin_ref[1] < done

    @pl.when(redo)
    def _():
        cmin_ref[0] = visit([(i, "diag"), (i - 1, "full")], True)[0]

    def cond(carry):
        j, cmin = carry
        return jnp.logical_and(j >= 0, cmin < done)

    def body(carry):
        j, _ = carry
        return j - 1, visit([(j, "full")], False)[0]

    lax.while_loop(cond, body, (jnp.where(redo, i - 2, i - 3), cmin_ref[0]))
    for p in range(pairs):
        o = jnp.where(head_a, o_scr[p, 0:n, :], o_scr[p, n:2 * n, :])
        cols = slice(p * n, (p + 1) * n)
        o_ref[:, cols] = (o * _silu(z_ref[:, cols].astype(F32))).astype(o_ref.dtype)


def _sb_attn(proj3):
    bsz, seq, _ = proj3.shape
    n = SB_BLOCK
    w = BRANCH_WIDTH
    base = 7
    u2 = _sb_constants()
    return pl.pallas_call(
        _sb_kernel,
        out_shape=jax.ShapeDtypeStruct((bsz, seq, w), BF16),
        grid=(bsz, seq // n),
        in_specs=[
            pl.BlockSpec((None, n, w), lambda b, i: (b, i, base)),
            pl.BlockSpec((None, seq, w), lambda b, i: (b, 0, base + 1)),
            pl.BlockSpec((None, seq, w), lambda b, i: (b, 0, base + 2)),
            pl.BlockSpec((None, n, w), lambda b, i: (b, i, base + 3)),
            pl.BlockSpec(u2.shape, lambda b, i: (0, 0)),
        ],
        out_specs=pl.BlockSpec((None, n, w), lambda b, i: (b, i, 0)),
        scratch_shapes=[pltpu.VMEM((w // n, 2 * n, n), BF16),
                        pltpu.VMEM((w // n, 2 * SB_TOP_ROWS, n), BF16),
                        pltpu.VMEM((w // n, 3, 2 * n, 2 * n), BF16),
                        pltpu.VMEM((w // n, 2 * n, n), F32),
                        pltpu.VMEM((w // n, 2 * n, n), F32),
                        pltpu.SMEM((2,), F32)],
        compiler_params=pltpu.CompilerParams(
            dimension_semantics=("arbitrary", "arbitrary"),
            vmem_limit_bytes=_mib(40)),
        name="stick_breaking",
    )(proj3, proj3, proj3, proj3, jnp.asarray(u2, BF16))


def _merge_kernel(ya_ref, yb_ref, yc_ref, g0, g1, g2, g3, g4, g5, x_ref, gate_ref, wb_ref,
                  wo_ref, fnw_ref, o_ref, *, final):
    w = BRANCH_WIDTH
    ys = (ya_ref[...], yb_ref[...], yc_ref[...])
    gl = ((g0, g1), (g2, g3), (g4, g5))
    halves = []
    for half in range(2):
        acc = None
        for nb in range(N_BRANCH):
            br = jnp.dot(ys[nb], wb_ref[nb, :, half * w:(half + 1) * w],
                         preferred_element_type=F32)
            term = jax.nn.sigmoid(gl[nb][half][...].astype(F32)) * br
            acc = term if acc is None else acc + term
        halves.append(acc.astype(BF16))
    merged = jnp.concatenate(halves, axis=1)
    out = jnp.dot(merged, wo_ref[...], preferred_element_type=F32)
    xn = x_ref[...] + gate_ref[...] * out
    if final:
        ms = jnp.mean(xn * xn, axis=-1, keepdims=True)
        xn = xn * lax.rsqrt(ms + EPS) * fnw_ref[...]
    o_ref[...] = xn


def _merge(ya, yb, yc, proj, x2, mod5, wb_bf16, wo_bf16, final_norm_w, layer, seq, final):
    m, d = x2.shape
    w = BRANCH_WIDTH
    tm = 512
    per_seq = seq // tm
    gate0 = 11
    ycol = lambda: pl.BlockSpec((tm, w), lambda i: (i, 0))
    gcol = lambda cb: pl.BlockSpec((tm, w), lambda i: (i, gate0 + cb))
    return pl.pallas_call(
        functools.partial(_merge_kernel, final=final),
        out_shape=jax.ShapeDtypeStruct((m, d), F32),
        grid=(m // tm,),
        in_specs=[
            ycol(), ycol(), ycol(),
            gcol(0), gcol(1), gcol(2), gcol(3), gcol(4), gcol(5),
            pl.BlockSpec((tm, d), lambda i: (i, 0)),
            pl.BlockSpec((None, None, None, 1, d), lambda i: (layer, i // per_seq, 2, 0, 0)),
            pl.BlockSpec((None, N_BRANCH, w, d), lambda i: (layer, 0, 0, 0)),
            pl.BlockSpec((None, d, d), lambda i: (layer, 0, 0)),
            pl.BlockSpec((1, d), lambda i: (0, 0)),
        ],
        out_specs=pl.BlockSpec((tm, d), lambda i: (i, 0)),
        compiler_params=pltpu.CompilerParams(
            dimension_semantics=("arbitrary",),
            vmem_limit_bytes=_mib(48)),
        name="merge_out",
    )(ya, yb, yc, proj, proj, proj, proj, proj, proj, x2, mod5, wb_bf16, wo_bf16,
      final_norm_w.reshape(1, d))


def kernel(x, c, ada_w, ada_b, norm_w, w_in, hgrn_lb, hgrn_norm_w, conv_w, conv_b, conv_ln_w,
           conv_ln_b, w_branch, w_out, final_norm_w):
    bsz, seq, d = x.shape
    depth = ada_w.shape[0]
    m = bsz * seq
    mod = _ada_mod(c, ada_w, ada_b)
    mod5 = mod.reshape(depth, bsz, 3, 1, d)
    w_in_b = w_in.astype(BF16)
    wb_b = w_branch.astype(BF16)
    wo_b = w_out.astype(BF16)
    x2 = x.reshape(m, d)
    for layer in range(depth):
        proj = _inproj(x2, mod5, norm_w, w_in_b, layer, seq)
        proj3 = proj.reshape(bsz, seq, proj.shape[-1])
        ya = _hgrn(proj3, hgrn_lb, hgrn_norm_w, layer)
        yb =_conv(proj3, conv_w, conv_b, conv_ln_w, conv_ln_b, layer)
        yc = _sb_attn(proj3)
        x2 = _merge(ya.reshape(m, -1), yb.reshape(m, -1), yc.reshape(m, -1), proj, x2, mod5,
                    wb_b, wo_b, final_norm_w, layer, seq, final=(layer == depth - 1))
    return x2.reshape(bsz, seq, d)
```

```python
import functools

import numpy as np
import jax
import jax.numpy as jnp
from jax import lax
from jax.experimental import pallas as pl
from jax.experimental.pallas import tpu as pltpu

F32 = jnp.float32
BF16 = jnp.bfloat16

LANES_V7X = 128
SUBLANES_V7X = 8
VMEM_BYTES_V7X = 64 * 1024 * 1024

EPS = 1e-6
TINY = 1e-30
LOG2E = 1.4426950408889634
BRANCH_WIDTH = 512
HGRN_HEADS = 4
HGRN_HEAD_DIM = BRANCH_WIDTH // HGRN_HEADS
HGRN_CHUNK = 64
HGRN_LEVELS = 6
_HGRN_MXU_LEVELS = 3
CONV_WIDTH = 31
CONV_HALO = 32
SB_HEADS = 8
SB_HEAD_DIM = BRANCH_WIDTH // SB_HEADS
SB_BLOCK = 128
SB_SKIP_LOG = -104.0
N_BRANCH = 3


def _mib(n):
    return int(n) * 1024 * 1024


def _silu(x):
    return x * jax.nn.sigmoid(x)


def _dot(a, b):
    return jnp.dot(a.astype(BF16), b.astype(BF16), preferred_element_type=F32)


def _dot_nt(a, b):
    return lax.dot_general(a.astype(BF16), b.astype(BF16), (((1,), (1,)), ((), ())),
                           preferred_element_type=F32)


def _dot_tn(a, b):
    return lax.dot_general(a.astype(BF16), b.astype(BF16), (((0,), (0,)), ((), ())),
                           preferred_element_type=F32)


def _split3(x):
    hi = x.astype(BF16)
    r = x - hi.astype(F32)
    mid = r.astype(BF16)
    lo = (r - mid.astype(F32)).astype(BF16)
    return hi, mid, lo


def _ada_kernel(c_ref, w_ref, b_ref, o_ref):
    c = c_ref[...]
    ca = _silu(c)
    w = w_ref[...]
    ch = ca.astype(BF16)
    cl = (ca - ch.astype(F32)).astype(BF16)
    wh = w.astype(BF16)
    wl = (w - wh.astype(F32)).astype(BF16)
    acc = (jnp.dot(ch, wh, preferred_element_type=F32)
           + jnp.dot(ch, wl, preferred_element_type=F32)
           + jnp.dot(cl, wh, preferred_element_type=F32))
    o_ref[...] = acc + b_ref[...]


def _ada_mod(c, ada_w, ada_b):
    depth, d, n = ada_w.shape
    bsz = c.shape[0]
    tn = n // 4
    return pl.pallas_call(
        _ada_kernel,
        out_shape=jax.ShapeDtypeStruct((depth, bsz, n), F32),
        grid=(depth, n // tn),
        in_specs=[
            pl.BlockSpec((bsz, d), lambda l, j: (0, 0)),
            pl.BlockSpec((None, d, tn), lambda l, j: (l, 0, j)),
            pl.BlockSpec((None, 1, tn), lambda l, j: (l, 0, j)),
        ],
        out_specs=pl.BlockSpec((None, bsz, tn), lambda l, j: (l, 0, j)),
        compiler_params=pltpu.CompilerParams(
            dimension_semantics=("arbitrary", "arbitrary"),
            vmem_limit_bytes=_mib(32)),
        name="ada_mod",
    )(c, ada_w, ada_b.reshape(depth, 1, n))


_INPROJ_NORM_ROWS = 256
_INPROJ_MM_ROWS = 512


def _inproj_kernel(x_ref, shift_ref, scale_ref, nw_ref, w_ref, o_ref, h_scr):
    tm = x_ref.shape[0]

    @pl.when(pl.program_id(1) == 0)
    def _():
        gain = nw_ref[...] * (1.0 + scale_ref[...])
        shift = shift_ref[...]

        def body(i, carry):
            r0 = pl.multiple_of(i * _INPROJ_NORM_ROWS, _INPROJ_NORM_ROWS)
            x = x_ref[pl.ds(r0, _INPROJ_NORM_ROWS), :]
            ms = jnp.mean(x * x, axis=-1, keepdims=True)
            h = (x * lax.rsqrt(ms + EPS)) * gain + shift
            h_scr[pl.ds(r0, _INPROJ_NORM_ROWS), :] = h.astype(BF16)
            return carry

        lax.fori_loop(0, tm // _INPROJ_NORM_ROWS, body, 0)

    def mm(i, carry):
        r0 = pl.multiple_of(i * _INPROJ_MM_ROWS, _INPROJ_MM_ROWS)
        o_ref[pl.ds(r0, _INPROJ_MM_ROWS), :] = jnp.dot(
            h_scr[pl.ds(r0, _INPROJ_MM_ROWS), :], w_ref[...],
            preferred_element_type=F32).astype(o_ref.dtype)
        return carry

    lax.fori_loop(0, tm // _INPROJ_MM_ROWS, mm, 0)


def _inproj(x2, mod5, norm_w, w_in_bf16, layer, seq):
    m, d = x2.shape
    n = w_in_bf16.shape[-1]
    tm = seq
    tn = n // 4
    per_seq = seq // tm
    return pl.pallas_call(
        _inproj_kernel,
        out_shape=jax.ShapeDtypeStruct((m, n), BF16),
        grid=(m // tm, n // tn),
        in_specs=[
            pl.BlockSpec((tm, d), lambda i, j: (i, 0)),
            pl.BlockSpec((None, None, None, 1, d), lambda i, j: (layer, i // per_seq, 0, 0, 0)),
            pl.BlockSpec((None, None, None, 1, d), lambda i, j: (layer, i // per_seq, 1, 0, 0)),
            pl.BlockSpec((None, 1, d), lambda i, j: (layer, 0, 0)),
            pl.BlockSpec((None, d, tn), lambda i, j: (layer, 0, j)),
        ],
        out_specs=pl.BlockSpec((tm, tn), lambda i, j: (i, j)),
        scratch_shapes=[pltpu.VMEM((tm, d), BF16)],
        compiler_params=pltpu.CompilerParams(
            dimension_semantics=("arbitrary", "arbitrary"),
            vmem_limit_bytes=_mib(56)),
        name="inproj",
    )(x2, mod5, mod5, norm_w.reshape(norm_w.shape[0], 1, d), w_in_bf16)


def _hgrn_constants():
    L = HGRN_CHUNK
    tri = np.tril(np.ones((L, L), np.float32))
    blocks = [tri]
    r = np.arange(L)
    for l in range(_HGRN_MXU_LEVELS):
        h = 1 << l
        p = (r // (2 * h)) * (2 * h) + h - 1
        blocks.append(tri - tri[p])
    c_all = np.concatenate(blocks, axis=0)
    c_all = np.concatenate([c_all, c_all, c_all], axis=1)
    t = r[:, None]
    s = r[None, :]
    x = t ^ s
    lev = np.where(x > 0, np.floor(np.log2(np.maximum(x, 1))).astype(np.int32), HGRN_LEVELS)
    lev = np.where(s > t, -1, lev).astype(np.int32)
    return c_all, lev


def _hgrn_kernel(q_ref, f_ref, i_ref, z_ref, lb_ref, nw_ref, c_ref, lev_ref, o_ref, st_ref,
                 *, layer):
    L = HGRN_CHUNK
    K = HGRN_HEAD_DIM
    n_chunks = q_ref.shape[0] // L

    @pl.when(pl.program_id(1) == 0)
    def _():
        st_ref[...] = jnp.zeros_like(st_ref)

    lb_all = lb_ref[...]
    lb_exp = jnp.exp(lb_all - jnp.max(lb_all, axis=0, keepdims=True))
    lb_soft = lb_exp / jnp.sum(lb_exp, axis=0, keepdims=True)
    lower = jnp.zeros((1, lb_all.shape[1]), F32)
    for l in range(1, layer + 1):
        lower = lower + lb_soft[l:l + 1, :]
    one_m_lower = 1.0 - lower
    nw = nw_ref[...]

    def body(c, carry):
        r0 = pl.multiple_of(c * L, L)
        rows = pl.ds(r0, L)
        q = _silu(q_ref[rows, :].astype(F32)) * (K ** -0.5)
        fr = f_ref[rows, :].astype(F32)
        f = lower + one_m_lower * jax.nn.sigmoid(fr)
        g = jnp.log(jnp.maximum(f, TINY)) * LOG2E
        k = one_m_lower * jax.nn.sigmoid(-fr)
        v = i_ref[rows, :]
        z = z_ref[rows, :].astype(F32)
        dall = jnp.dot(c_ref[...], jnp.concatenate(_split3(g), axis=0),
                       preferred_element_type=F32)
        lev = lev_ref[...]
        heads = range(HGRN_HEADS)
        hcols = [slice(h * K, (h + 1) * K) for h in heads]

        def boundary_diff(b, l, cols):
            if l < _HGRN_MXU_LEVELS:
                return dall[(l + 1) * L:(l + 2) * L, cols]
            hw = 1 << l
            parts = [jnp.broadcast_to(b[s + hw - 1:s + hw, :], (2 * hw, K))
                     for s in range(0, L, 2 * hw)]
            return b - (parts[0] if len(parts) == 1 else jnp.concatenate(parts, axis=0))

        pair, o_inter, upd, decay = [], [], [], []
        for h in heads:
            cols = hcols[h]
            qh, kh, vh = q[:, cols], k[:, cols], v[:, cols]
            b = dall[0:L, cols]
            b_last = b[L - 1:L, :]
            o_inter.append(_dot_nt(qh * jnp.exp2(b), st_ref[h]))
            upd.append(_dot_tn(vh, kh * jnp.exp2(b_last - b)))
            decay.append(jnp.exp2(b_last))
            prods = [_dot_nt(qh, kh)]
            for l in range(HGRN_LEVELS):
                fac = jnp.exp2(-jnp.abs(boundary_diff(b, l, cols)))
                prods.append(_dot_nt(qh * fac, kh * fac))
            pair.append(prods)
        outs = []
        for h in heads:
            a = jnp.where(lev == HGRN_LEVELS, pair[h][0], 0.0)
            for l in range(HGRN_LEVELS):
                a = jnp.where(lev == l, pair[h][l + 1], a)
            outs.append(o_inter[h] + _dot(a, v[:, hcols[h]]))
        for h in heads:
            st_ref[h] = st_ref[h] * decay[h] + upd[h]
            o = outs[h]
            on = o * lax.rsqrt(jnp.mean(o * o, axis=-1, keepdims=True) + EPS) * nw
            o_ref[rows, hcols[h]] = (on * _silu(z[:, hcols[h]])).astype(o_ref.dtype)
        return carry

    lax.fori_loop(0, n_chunks, body, 0, unroll=4)


def _hgrn(proj3, hgrn_lb, hgrn_norm_w, layer):
    bsz, seq, _ = proj3.shape
    w = BRANCH_WIDTH
    tb = 512
    c_all, lev = _hgrn_constants()
    col = lambda cb: pl.BlockSpec((None, tb, w), lambda b, s: (b, s, cb))
    return pl.pallas_call(
        functools.partial(_hgrn_kernel, layer=layer),
        out_shape=jax.ShapeDtypeStruct((bsz, seq, w), BF16),
        grid=(bsz, seq // tb),
        in_specs=[
            col(0), col(1), col(2), col(3),
            pl.BlockSpec(hgrn_lb.shape, lambda b, s: (0, 0)),
            pl.BlockSpec((None, 1, HGRN_HEAD_DIM), lambda b, s: (layer, 0, 0)),
            pl.BlockSpec(c_all.shape, lambda b, s: (0, 0)),
            pl.BlockSpec(lev.shape, lambda b, s: (0, 0)),
        ],
        out_specs=pl.BlockSpec((None, tb, w), lambda b, s: (b, s, 0)),
        scratch_shapes=[pltpu.VMEM((HGRN_HEADS, HGRN_HEAD_DIM, HGRN_HEAD_DIM), F32)],
        compiler_params=pltpu.CompilerParams(
            dimension_semantics=("arbitrary", "arbitrary"),
            vmem_limit_bytes=_mib(32)),
        name="hgrn2",
    )(proj3, proj3, proj3, proj3, hgrn_lb,
      hgrn_norm_w.reshape(hgrn_norm_w.shape[0], 1, HGRN_HEAD_DIM),
      jnp.asarray(c_all, BF16), jnp.asarray(lev))


_CONV_ROWS = 32
_CONV_GLU_ROWS = 64


def _conv_kernel(a_ref, g_ref, z_ref, w_ref, b_ref, lnw_ref, lnb_ref, o_ref, u_scr, wb_scr):
    ts = a_ref.shape[0]
    sub = SUBLANES_V7X

    @pl.when(pl.program_id(1) == 0)
    def _():
        u_scr[:, 0:CONV_HALO, :] = jnp.zeros((sub, CONV_HALO, u_scr.shape[2]), F32)

    @pl.when(pl.program_id(1) > 0)
    def _():
        for res in range(sub):
            u_scr[res, 0:CONV_HALO - res, :] = u_scr[res, ts:ts + CONV_HALO - res, :]

    for c in range(ts // _CONV_GLU_ROWS):
        r0 = c * _CONV_GLU_ROWS
        u = (a_ref[r0:r0 + _CONV_GLU_ROWS, :].astype(F32)
             * jax.nn.sigmoid(g_ref[r0:r0 + _CONV_GLU_ROWS, :].astype(F32)))
        for res in range(sub):
            u_scr[res, CONV_HALO - res + r0:CONV_HALO - res + r0 + _CONV_GLU_ROWS, :] = u

    wd = w_ref.shape[1]
    for j in range(CONV_WIDTH):
        wb_scr[j] = jnp.broadcast_to(w_ref[j:j + 1, :], (sub, wd))
    wb_scr[CONV_WIDTH] = jnp.broadcast_to(b_ref[...], (sub, wd))
    lnw = lnw_ref[...]
    lnb = lnb_ref[...]
    first = CONV_HALO - (CONV_WIDTH - 1)
    tiles = _CONV_ROWS // sub

    def chunk(c, carry):
        r0 = pl.multiple_of(c * _CONV_ROWS, _CONV_ROWS)
        acc = jnp.broadcast_to(wb_scr[CONV_WIDTH][None], (tiles, sub, wd))
        for j in range(CONV_WIDTH):
            res = (first + j) % sub
            win = u_scr[res, pl.ds(r0 + (first + j - res), _CONV_ROWS), :]
            acc = acc + wb_scr[j][None] * win.reshape(tiles, sub, wd)
        acc = acc.reshape(_CONV_ROWS, wd)
        mu = jnp.mean(acc, axis=-1, keepdims=True)
        d = acc - mu
        var = jnp.mean(d * d, axis=-1, keepdims=True)
        y = _silu(d * lax.rsqrt(var + EPS) * lnw + lnb)
        z = z_ref[pl.ds(r0, _CONV_ROWS), :].astype(F32)
        o_ref[pl.ds(r0, _CONV_ROWS), :] = (y * _silu(z)).astype(o_ref.dtype)
        return carry

    lax.fori_loop(0, ts // _CONV_ROWS, chunk, 0, unroll=4)


def _conv(proj3, conv_w, conv_b, ln_w, ln_b, layer):
    bsz, seq, _ = proj3.shape
    w = BRANCH_WIDTH
    ts = 256
    col = lambda cb: pl.BlockSpec((None, ts, w), lambda b, s: (b, s, cb))
    vec = lambda: pl.BlockSpec((None, 1, w), lambda b, s: (layer, 0, 0))
    depth = conv_w.shape[0]
    return pl.pallas_call(
        _conv_kernel,
        out_shape=jax.ShapeDtypeStruct((bsz, seq, w), BF16),
        grid=(bsz, seq // ts),
        in_specs=[
            col(4), col(5), col(6),
            pl.BlockSpec((None, CONV_WIDTH, w), lambda b, s: (layer, 0, 0)),
            vec(), vec(), vec(),
        ],
        out_specs=pl.BlockSpec((None, ts, w), lambda b, s: (b, s, 0)),
        scratch_shapes=[pltpu.VMEM((SUBLANES_V7X, CONV_HALO + ts, w), F32),
                        pltpu.VMEM((CONV_WIDTH + 1, SUBLANES_V7X, w), F32)],
        compiler_params=pltpu.CompilerParams(
            dimension_semantics=("arbitrary", "arbitrary"),
            vmem_limit_bytes=_mib(32)),
        name="conv_module",
    )(proj3, proj3, proj3, conv_w, conv_b.reshape(depth, 1, w), ln_w.reshape(depth, 1, w),
      ln_b.reshape(depth, 1, w))


def _sb_constants():
    n = SB_BLOCK
    u = (np.arange(n)[:, None] >= np.arange(n)[None, :]).astype(np.float32)
    return np.concatenate([u, np.ones((n, n), np.float32)], axis=1)


def _sb_kernel(q_ref, k_ref, v_ref, z_ref, u_ref, o_ref, q_scr, c_scr, o_scr, cmin_ref):
    n = SB_BLOCK
    pairs = q_ref.shape[1] // n
    prs = range(pairs)
    i = pl.program_id(1)
    lane = lax.broadcasted_iota(jnp.int32, (2 * n, n), 1)
    row = lax.broadcasted_iota(jnp.int32, (2 * n, n), 0)
    strict = lane < (row & (n - 1))
    head_a = lax.broadcasted_iota(jnp.int32, (n, n), 1) < SB_HEAD_DIM

    for p in prs:
        q = q_ref[:, p * n:(p + 1) * n].astype(F32) * (SB_HEAD_DIM ** -0.5)
        q_scr[p, 0:n, :] = jnp.where(head_a, q, 0.0).astype(BF16)
        q_scr[p, n:2 * n, :] = jnp.where(head_a, 0.0, q).astype(BF16)

    def visit(blocks, fresh):
        units = [(p, t) for p in prs for t in range(len(blocks))]
        kv = {}
        for t, (j, _) in enumerate(blocks):
            rows = pl.ds(pl.multiple_of(j * n, n), n)
            for p in prs:
                kv[p, t] = (k_ref[rows, p * n:(p + 1) * n], v_ref[rows, p * n:(p + 1) * n])
        logits = {u: _dot_nt(q_scr[u[0]], kv[u][0]) for u in units}
        sums = {}
        for u in units:
            p, t = u
            x = logits[u]
            drop = jnp.maximum(x, 0.0) + jnp.log(1.0 + jnp.exp2(jnp.abs(x) * (-LOG2E)))
            if blocks[t][1]:
                drop = jnp.where(strict, drop, 0.0)
            sums[u] = _dot(drop, u_ref[...])
        cmin = None
        pvs = []
        for p in prs:
            c = None if fresh else c_scr[p]
            pv = None
            for t in range(len(blocks)):
                r = sums[p, t]
                total = r[:, :n] if c is None else r[:, :n] + c
                w = jnp.exp(logits[p, t] - total)
                if blocks[t][1]:
                    w = jnp.where(strict, w, 0.0)
                term = _dot(w, kv[p, t][1])
                pv = term if pv is None else pv + term
                c = r[:, n:] if c is None else c + r[:, n:]
            c_scr[p] = c
            pvs.append(pv)
            cmin = c if cmin is None else jnp.minimum(cmin, c)
        for p in prs:
            o_scr[p] = pvs[p] if fresh else o_scr[p] + pvs[p]
        return jnp.min(cmin)

    @pl.when(i == 0)
    def _():
        cmin_ref[0] = visit([(i, True)], True)

    @pl.when(i == 1)
    def _():
        cmin_ref[0] = visit([(i, True), (i - 1, False)], True)

    @pl.when(i >= 2)
    def _():
        cmin_ref[0] = visit([(i, True), (i - 1, False), (i - 2, False)], True)

    def cond(carry):
        j, cmin = carry
        return jnp.logical_and(j >= 0, cmin < -SB_SKIP_LOG)

    def body(carry):
        j, _ = carry
        return j - 1, visit([(j, False)], False)

    lax.while_loop(cond, body, (i - 3, cmin_ref[0]))
    for p in range(pairs):
        o = jnp.where(head_a, o_scr[p, 0:n, :], o_scr[p, n:2 * n, :])
        cols = slice(p * n, (p + 1) * n)
        o_ref[:, cols] = (o * _silu(z_ref[:, cols].astype(F32))).astype(o_ref.dtype)


def _sb_attn(proj3):
    bsz, seq, _ = proj3.shape
    n = SB_BLOCK
    w = BRANCH_WIDTH
    base = 7
    u2 = _sb_constants()
    return pl.pallas_call(
        _sb_kernel,
        out_shape=jax.ShapeDtypeStruct((bsz, seq, w), BF16),
        grid=(bsz, seq // n),
        in_specs=[
            pl.BlockSpec((None, n, w), lambda b, i: (b, i, base)),
            pl.BlockSpec((None, seq, w), lambda b, i: (b, 0, base + 1)),
            pl.BlockSpec((None, seq, w), lambda b, i: (b, 0, base + 2)),
            pl.BlockSpec((None, n, w), lambda b, i: (b, i, base + 3)),
            pl.BlockSpec(u2.shape, lambda b, i: (0, 0)),
        ],
        out_specs=pl.BlockSpec((None, n, w), lambda b, i: (b, i, 0)),
        scratch_shapes=[pltpu.VMEM((w // n, 2 * n, n), BF16),
                        pltpu.VMEM((w // n, 2 * n, n), F32),
                        pltpu.VMEM((w // n, 2 * n, n), F32),
                        pltpu.SMEM((1,), F32)],
        compiler_params=pltpu.CompilerParams(
            dimension_semantics=("arbitrary", "arbitrary"),
            vmem_limit_bytes=_mib(40)),
        name="stick_breaking",
    )(proj3, proj3, proj3, proj3, jnp.asarray(u2, BF16))


def _merge_kernel(ya_ref, yb_ref, yc_ref, g0, g1, g2, g3, g4, g5, x_ref, gate_ref, wb_ref,
                  wo_ref, fnw_ref, o_ref, *, final):
    w = BRANCH_WIDTH
    ys = (ya_ref[...], yb_ref[...], yc_ref[...])
    gl = ((g0, g1), (g2, g3), (g4, g5))
    halves = []
    for half in range(2):
        acc = None
        for nb in range(N_BRANCH):
            br = jnp.dot(ys[nb], wb_ref[nb, :, half * w:(half + 1) * w],
                         preferred_element_type=F32)
            term = jax.nn.sigmoid(gl[nb][half][...].astype(F32)) * br
            acc = term if acc is None else acc + term
        halves.append(acc.astype(BF16))
    merged = jnp.concatenate(halves, axis=1)
    out = jnp.dot(merged, wo_ref[...], preferred_element_type=F32)
    xn = x_ref[...] + gate_ref[...] * out
    if final:
        ms = jnp.mean(xn * xn, axis=-1, keepdims=True)
        xn = xn * lax.rsqrt(ms + EPS) * fnw_ref[...]
    o_ref[...] = xn


def _merge(ya, yb, yc, proj, x2, mod5, wb_bf16, wo_bf16, final_norm_w, layer, seq, final):
    m, d = x2.shape
    w = BRANCH_WIDTH
    tm = 512
    per_seq = seq // tm
    gate0 = 11
    ycol = lambda: pl.BlockSpec((tm, w), lambda i: (i, 0))
    gcol = lambda cb: pl.BlockSpec((tm, w), lambda i: (i, gate0 + cb))
    return pl.pallas_call(
        functools.partial(_merge_kernel, final=final),
        out_shape=jax.ShapeDtypeStruct((m, d), F32),
        grid=(m // tm,),
        in_specs=[
            ycol(), ycol(), ycol(),
            gcol(0), gcol(1), gcol(2), gcol(3), gcol(4), gcol(5),
            pl.BlockSpec((tm, d), lambda i: (i, 0)),
            pl.BlockSpec((None, None, None, 1, d), lambda i: (layer, i // per_seq, 2, 0, 0)),
            pl.BlockSpec((None, N_BRANCH, w, d), lambda i: (layer, 0, 0, 0)),
            pl.BlockSpec((None, d, d), lambda i: (layer, 0, 0)),
            pl.BlockSpec((1, d), lambda i: (0, 0)),
        ],
        out_specs=pl.BlockSpec((tm, d), lambda i: (i, 0)),
        compiler_params=pltpu.CompilerParams(
            dimension_semantics=("arbitrary",),
            vmem_limit_bytes=_mib(48)),
        name="merge_out",
    )(ya, yb, yc, proj, proj, proj, proj, proj, proj, x2, mod5, wb_bf16, wo_bf16,
      final_norm_w.reshape(1, d))


def kernel(x, c, ada_w, ada_b, norm_w, w_in, hgrn_lb, hgrn_norm_w, conv_w, conv_b, conv_ln_w,
           conv_ln_b, w_branch, w_out, final_norm_w):
    bsz, seq, d = x.shape
    depth = ada_w.shape[0]
    m = bsz * seq
    mod = _ada_mod(c, ada_w, ada_b)
    mod5 = mod.reshape(depth, bsz, 3, 1, d)
    w_in_b = w_in.astype(BF16)
    wb_b = w_branch.astype(BF16)
    wo_b = w_out.astype(BF16)
    x2 = x.reshape(m, d)
    for layer in range(depth):
        proj = _inproj(x2, mod5, norm_w, w_in_b, layer, seq)
        proj3 = proj.reshape(bsz, seq, proj.shape[-1])
        ya = _hgrn(proj3, hgrn_lb, hgrn_norm_w, layer)
        yb =_conv(proj3, conv_w, conv_b, conv_ln_w, conv_ln_b, layer)
        yc = _sb_attn(proj3)
        x2 = _merge(ya.reshape(m, -1), yb.reshape(m, -1), yc.reshape(m, -1), proj, x2, mod5,
                    wb_b, wo_b, final_norm_w, layer, seq, final=(layer == depth - 1))
    return x2.reshape(bsz, seq, d)
```

```python
import functools

import numpy as np
import jax
import jax.numpy as jnp
from jax import lax
from jax.experimental import pallas as pl
from jax.experimental.pallas import tpu as pltpu

F32 = jnp.float32
BF16 = jnp.bfloat16

LANES_V7X = 128
SUBLANES_V7X = 8
VMEM_BYTES_V7X = 64 * 1024 * 1024

EPS = 1e-6
TINY = 1e-30
LOG2E = 1.4426950408889634
BRANCH_WIDTH = 512
HGRN_HEADS = 4
HGRN_HEAD_DIM = BRANCH_WIDTH // HGRN_HEADS
HGRN_CHUNK = 64
HGRN_LEVELS = 6
_HGRN_MXU_LEVELS = 3
CONV_WIDTH = 31
CONV_HALO = 32
SB_HEADS = 8
SB_HEAD_DIM = BRANCH_WIDTH // SB_HEADS
SB_BLOCK = 128
SB_SKIP_LOG = -104.0
N_BRANCH = 3


def _mib(n):
    return int(n) * 1024 * 1024


def _silu(x):
    return x * jax.nn.sigmoid(x)


def _dot(a, b):
    return jnp.dot(a.astype(BF16), b.astype(BF16), preferred_element_type=F32)


def _dot_nt(a, b):
    return lax.dot_general(a.astype(BF16), b.astype(BF16), (((1,), (1,)), ((), ())),
                           preferred_element_type=F32)


def _dot_tn(a, b):
    return lax.dot_general(a.astype(BF16), b.astype(BF16), (((0,), (0,)), ((), ())),
                           preferred_element_type=F32)


def _split3(x):
    hi = x.astype(BF16)
    r = x - hi.astype(F32)
    mid = r.astype(BF16)
    lo = (r - mid.astype(F32)).astype(BF16)
    return hi, mid, lo


def _ada_kernel(c_ref, w_ref, b_ref, o_ref):
    c = c_ref[...]
    ca = _silu(c)
    w = w_ref[...]
    ch = ca.astype(BF16)
    cl = (ca - ch.astype(F32)).astype(BF16)
    wh = w.astype(BF16)
    wl = (w - wh.astype(F32)).astype(BF16)
    acc = (jnp.dot(ch, wh, preferred_element_type=F32)
           + jnp.dot(ch, wl, preferred_element_type=F32)
           + jnp.dot(cl, wh, preferred_element_type=F32))
    o_ref[...] = acc + b_ref[...]


def _ada_mod(c, ada_w, ada_b):
    depth, d, n = ada_w.shape
    bsz = c.shape[0]
    tn = n // 4
    return pl.pallas_call(
        _ada_kernel,
        out_shape=jax.ShapeDtypeStruct((depth, bsz, n), F32),
        grid=(depth, n // tn),
        in_specs=[
            pl.BlockSpec((bsz, d), lambda l, j: (0, 0)),
            pl.BlockSpec((None, d, tn), lambda l, j: (l, 0, j)),
            pl.BlockSpec((None, 1, tn), lambda l, j: (l, 0, j)),
        ],
        out_specs=pl.BlockSpec((None, bsz, tn), lambda l, j: (l, 0, j)),
        compiler_params=pltpu.CompilerParams(
            dimension_semantics=("arbitrary", "arbitrary"),
            vmem_limit_bytes=_mib(32)),
        name="ada_mod",
    )(c, ada_w, ada_b.reshape(depth, 1, n))


_INPROJ_NORM_ROWS = 256
_INPROJ_MM_ROWS = 1024


def _inproj_kernel(x_ref, shift_ref, scale_ref, nw_ref, w_ref, o_ref, h_scr):
    tm = x_ref.shape[0]

    @pl.when(pl.program_id(1) == 0)
    def _():
        gain = nw_ref[...] * (1.0 + scale_ref[...])
        shift = shift_ref[...]

        def body(i, carry):
            r0 = pl.multiple_of(i * _INPROJ_NORM_ROWS, _INPROJ_NORM_ROWS)
            x = x_ref[pl.ds(r0, _INPROJ_NORM_ROWS), :]
            ms = jnp.mean(x * x, axis=-1, keepdims=True)
            h = (x * lax.rsqrt(ms + EPS)) * gain + shift
            h_scr[pl.ds(r0, _INPROJ_NORM_ROWS), :] = h.astype(BF16)
            return carry

        lax.fori_loop(0, tm // _INPROJ_NORM_ROWS, body, 0)

    def mm(i, carry):
        r0 = pl.multiple_of(i * _INPROJ_MM_ROWS, _INPROJ_MM_ROWS)
        o_ref[pl.ds(r0, _INPROJ_MM_ROWS), :] = jnp.dot(
            h_scr[pl.ds(r0, _INPROJ_MM_ROWS), :], w_ref[...],
            preferred_element_type=F32).astype(o_ref.dtype)
        return carry

    lax.fori_loop(0, tm // _INPROJ_MM_ROWS, mm, 0)


def _inproj(x2, mod5, norm_w, w_in_bf16, layer, seq):
    m, d = x2.shape
    n = w_in_bf16.shape[-1]
    tm = seq
    tn = n // 4
    per_seq = seq // tm
    return pl.pallas_call(
        _inproj_kernel,
        out_shape=jax.ShapeDtypeStruct((m, n), BF16),
        grid=(m // tm, n // tn),
        in_specs=[
            pl.BlockSpec((tm, d), lambda i, j: (i, 0)),
            pl.BlockSpec((None, None, None, 1, d), lambda i, j: (layer, i // per_seq, 0, 0, 0)),
            pl.BlockSpec((None, None, None, 1, d), lambda i, j: (layer, i // per_seq, 1, 0, 0)),
            pl.BlockSpec((None, 1, d), lambda i, j: (layer, 0, 0)),
            pl.BlockSpec((None, d, tn), lambda i, j: (layer, 0, j)),
        ],
        out_specs=pl.BlockSpec((tm, tn), lambda i, j: (i, j)),
        scratch_shapes=[pltpu.VMEM((tm, d), BF16)],
        compiler_params=pltpu.CompilerParams(
            dimension_semantics=("arbitrary", "arbitrary"),
            vmem_limit_bytes=_mib(56)),
        name="inproj",
    )(x2, mod5, mod5, norm_w.reshape(norm_w.shape[0], 1, d), w_in_bf16)


def _hgrn_constants():
    L = HGRN_CHUNK
    tri = np.tril(np.ones((L, L), np.float32))
    blocks = [tri]
    r = np.arange(L)
    for l in range(_HGRN_MXU_LEVELS):
        h = 1 << l
        p = (r // (2 * h)) * (2 * h) + h - 1
        sign = np.where(r > p, 1.0, -1.0).astype(np.float32)[:, None]
        blocks.append(sign * (tri - tri[p]))
    c_all = np.concatenate(blocks, axis=0)
    c_all = np.concatenate([c_all, c_all, c_all], axis=1)
    t = r[:, None]
    s = r[None, :]
    x = t ^ s
    lev = np.where(x > 0, np.floor(np.log2(np.maximum(x, 1))).astype(np.int32), HGRN_LEVELS)
    lev = np.where(s > t, -1, lev).astype(np.int32)
    return c_all, lev


def _hgrn_kernel(q_ref, f_ref, i_ref, z_ref, lb_ref, nw_ref, c_ref, lev_ref, o_ref, st_ref,
                 *, layer):
    L = HGRN_CHUNK
    K = HGRN_HEAD_DIM
    n_chunks = q_ref.shape[0] // L

    @pl.when(pl.program_id(1) == 0)
    def _():
        st_ref[...] = jnp.zeros_like(st_ref)

    lb_all = lb_ref[...]
    lb_exp = jnp.exp(lb_all - jnp.max(lb_all, axis=0, keepdims=True))
    lb_soft = lb_exp / jnp.sum(lb_exp, axis=0, keepdims=True)
    lower = jnp.zeros((1, lb_all.shape[1]), F32)
    for l in range(1, layer + 1):
        lower = lower + lb_soft[l:l + 1, :]
    one_m_lower = 1.0 - lower
    nw = nw_ref[...]

    def body(c, carry):
        r0 = pl.multiple_of(c * L, L)
        rows = pl.ds(r0, L)
        q = _silu(q_ref[rows, :].astype(F32)) * (K ** -0.5)
        fr = f_ref[rows, :].astype(F32)
        f = lower + one_m_lower * jax.nn.sigmoid(fr)
        g = jnp.log(jnp.maximum(f, TINY)) * LOG2E
        k = one_m_lower * jax.nn.sigmoid(-fr)
        v = i_ref[rows, :]
        z = z_ref[rows, :].astype(F32)
        dall = jnp.dot(c_ref[...], jnp.concatenate(_split3(g), axis=0),
                       preferred_element_type=F32)
        lev = lev_ref[...]
        heads = range(HGRN_HEADS)
        hcols = [slice(h * K, (h + 1) * K) for h in heads]

        def boundary_gap(b, l, cols):
            if l < _HGRN_MXU_LEVELS:
                return dall[(l + 1) * L:(l + 2) * L, cols]
            hw = 1 << l
            parts = []
            for s in range(0, L, 2 * hw):
                bp = b[s + hw - 1:s + hw, :]
                parts += [bp - b[s:s + hw, :], b[s + hw:s + 2 * hw, :] - bp]
            return jnp.concatenate(parts, axis=0)

        pair, o_inter, upd, decay = [], [], [], []
        for h in heads:
            cols = hcols[h]
            qh, kh, vh = q[:, cols], k[:, cols], v[:, cols]
            b = dall[0:L, cols]
            b_last = b[L - 1:L, :]
            o_inter.append(_dot_nt(qh * jnp.exp2(b), st_ref[h]))
            upd.append(_dot_tn(vh, kh * jnp.exp2(b_last - b)))
            decay.append(jnp.exp2(b_last))
            prods = [_dot_nt(qh, kh)]
            for l in range(HGRN_LEVELS):
                fac = jnp.exp2(boundary_gap(b, l, cols))
                prods.append(_dot_nt(qh * fac, kh * fac))
            pair.append(prods)
        outs = []
        for h in heads:
            a = jnp.where(lev == HGRN_LEVELS, pair[h][0], 0.0)
            for l in range(HGRN_LEVELS):
                a = jnp.where(lev == l, pair[h][l + 1], a)
            outs.append(o_inter[h] + _dot(a, v[:, hcols[h]]))
        for h in heads:
            st_ref[h] = st_ref[h] * decay[h] + upd[h]
            o = outs[h]
            on = o * lax.rsqrt(jnp.mean(o * o, axis=-1, keepdims=True) + EPS) * nw
            o_ref[rows, hcols[h]] = (on * _silu(z[:, hcols[h]])).astype(o_ref.dtype)
        return carry

    lax.fori_loop(0, n_chunks, body, 0, unroll=4)


def _hgrn(proj3, hgrn_lb, hgrn_norm_w, layer):
    bsz, seq, _ = proj3.shape
    w = BRANCH_WIDTH
    tb = 512
    c_all, lev = _hgrn_constants()
    col = lambda cb: pl.BlockSpec((None, tb, w), lambda b, s: (b, s, cb))
    return pl.pallas_call(
        functools.partial(_hgrn_kernel, layer=layer),
        out_shape=jax.ShapeDtypeStruct((bsz, seq, w), BF16),
        grid=(bsz, seq // tb),
        in_specs=[
            col(0), col(1), col(2), col(3),
            pl.BlockSpec(hgrn_lb.shape, lambda b, s: (0, 0)),
            pl.BlockSpec((None, 1, HGRN_HEAD_DIM), lambda b, s: (layer, 0, 0)),
            pl.BlockSpec(c_all.shape, lambda b, s: (0, 0)),
            pl.BlockSpec(lev.shape, lambda b, s: (0, 0)),
        ],
        out_specs=pl.BlockSpec((None, tb, w), lambda b, s: (b, s, 0)),
        scratch_shapes=[pltpu.VMEM((HGRN_HEADS, HGRN_HEAD_DIM, HGRN_HEAD_DIM), F32)],
        compiler_params=pltpu.CompilerParams(
            dimension_semantics=("arbitrary", "arbitrary"),
            vmem_limit_bytes=_mib(32)),
        name="hgrn2",
    )(proj3, proj3, proj3, proj3, hgrn_lb,
      hgrn_norm_w.reshape(hgrn_norm_w.shape[0], 1, HGRN_HEAD_DIM),
      jnp.asarray(c_all, BF16), jnp.asarray(lev))


_CONV_ROWS = 32
_CONV_GLU_ROWS = 64


def _conv_kernel(a_ref, g_ref, z_ref, w_ref, b_ref, lnw_ref, lnb_ref, o_ref, u_scr, wb_scr):
    ts = a_ref.shape[0]
    sub = SUBLANES_V7X

    @pl.when(pl.program_id(1) == 0)
    def _():
        u_scr[:, 0:CONV_HALO, :] = jnp.zeros((sub, CONV_HALO, u_scr.shape[2]), F32)

    @pl.when(pl.program_id(1) > 0)
    def _():
        for res in range(sub):
            u_scr[res, 0:CONV_HALO - res, :] = u_scr[res, ts:ts + CONV_HALO - res, :]

    for c in range(ts // _CONV_GLU_ROWS):
        r0 = c * _CONV_GLU_ROWS
        u = (a_ref[r0:r0 + _CONV_GLU_ROWS, :].astype(F32)
             * jax.nn.sigmoid(g_ref[r0:r0 + _CONV_GLU_ROWS, :].astype(F32)))
        for res in range(sub):
            u_scr[res, CONV_HALO - res + r0:CONV_HALO - res + r0 + _CONV_GLU_ROWS, :] = u

    wd = w_ref.shape[1]
    for j in range(CONV_WIDTH):
        wb_scr[j] = jnp.broadcast_to(w_ref[j:j + 1, :], (sub, wd))
    wb_scr[CONV_WIDTH] = jnp.broadcast_to(b_ref[...], (sub, wd))
    lnw = lnw_ref[...]
    lnb = lnb_ref[...]
    first = CONV_HALO - (CONV_WIDTH - 1)
    tiles = _CONV_ROWS // sub

    def chunk(c, carry):
        r0 = pl.multiple_of(c * _CONV_ROWS, _CONV_ROWS)
        acc = jnp.broadcast_to(wb_scr[CONV_WIDTH][None], (tiles, sub, wd))
        for j in range(CONV_WIDTH):
            res = (first + j) % sub
            win = u_scr[res, pl.ds(r0 + (first + j - res), _CONV_ROWS), :]
            acc = acc + wb_scr[j][None] * win.reshape(tiles, sub, wd)
        acc = acc.reshape(_CONV_ROWS, wd)
        mu = jnp.mean(acc, axis=-1, keepdims=True)
        d = acc - mu
        var = jnp.mean(d * d, axis=-1, keepdims=True)
        y = _silu(d * lax.rsqrt(var + EPS) * lnw + lnb)
        z = z_ref[pl.ds(r0, _CONV_ROWS), :].astype(F32)
        o_ref[pl.ds(r0, _CONV_ROWS), :] = (y * _silu(z)).astype(o_ref.dtype)
        return carry

    lax.fori_loop(0, ts // _CONV_ROWS, chunk, 0, unroll=4)


def _conv(proj3, conv_w, conv_b, ln_w, ln_b, layer):
    bsz, seq, _ = proj3.shape
    w = BRANCH_WIDTH
    ts = 256
    col = lambda cb: pl.BlockSpec((None, ts, w), lambda b, s: (b, s, cb))
    vec = lambda: pl.BlockSpec((None, 1, w), lambda b, s: (layer, 0, 0))
    depth = conv_w.shape[0]
    return pl.pallas_call(
        _conv_kernel,
        out_shape=jax.ShapeDtypeStruct((bsz, seq, w), BF16),
        grid=(bsz, seq // ts),
        in_specs=[
            col(4), col(5), col(6),
            pl.BlockSpec((None, CONV_WIDTH, w), lambda b, s: (layer, 0, 0)),
            vec(), vec(), vec(),
        ],
        out_specs=pl.BlockSpec((None, ts, w), lambda b, s: (b, s, 0)),
        scratch_shapes=[pltpu.VMEM((SUBLANES_V7X, CONV_HALO + ts, w), F32),
                        pltpu.VMEM((CONV_WIDTH + 1, SUBLANES_V7X, w), F32)],
        compiler_params=pltpu.CompilerParams(
            dimension_semantics=("arbitrary", "arbitrary"),
            vmem_limit_bytes=_mib(32)),
        name="conv_module",
    )(proj3, proj3, proj3, conv_w, conv_b.reshape(depth, 1, w), ln_w.reshape(depth, 1, w),
      ln_b.reshape(depth, 1, w))


def _sb_constants():
    n = SB_BLOCK
    u = (np.arange(n)[:, None] >= np.arange(n)[None, :]).astype(np.float32)
    return np.concatenate([u, np.ones((n, n), np.float32)], axis=1)


def _sb_kernel(q_ref, k_ref, v_ref, z_ref, u_ref, o_ref, q_scr, c_scr, o_scr, cmin_ref):
    n = SB_BLOCK
    pairs = q_ref.shape[1] // n
    prs = range(pairs)
    i = pl.program_id(1)
    lane = lax.broadcasted_iota(jnp.int32, (2 * n, n), 1)
    row = lax.broadcasted_iota(jnp.int32, (2 * n, n), 0)
    strict = lane < (row & (n - 1))
    head_a = lax.broadcasted_iota(jnp.int32, (n, n), 1) < SB_HEAD_DIM

    for p in prs:
        q = q_ref[:, p * n:(p + 1) * n].astype(F32) * (SB_HEAD_DIM ** -0.5)
        q_scr[p, 0:n, :] = jnp.where(head_a, q, 0.0).astype(BF16)
        q_scr[p, n:2 * n, :] = jnp.where(head_a, 0.0, q).astype(BF16)

    def visit(blocks, fresh):
        units = [(p, t) for p in prs for t in range(len(blocks))]
        kv = {}
        for t, (j, _) in enumerate(blocks):
            rows = pl.ds(pl.multiple_of(j * n, n), n)
            for p in prs:
                kv[p, t] = (k_ref[rows, p * n:(p + 1) * n], v_ref[rows, p * n:(p + 1) * n])
        logits = {u: _dot_nt(q_scr[u[0]], kv[u][0]) for u in units}
        sums = {}
        for u in units:
            p, t = u
            x = logits[u]
            drop = jnp.maximum(x, 0.0) + jnp.log(1.0 + jnp.exp2(jnp.abs(x) * (-LOG2E)))
            if blocks[t][1]:
                drop = jnp.where(strict, drop, 0.0)
            sums[u] = _dot(drop, u_ref[...])
        cmin = None
        pvs = []
        for p in prs:
            c = None if fresh else c_scr[p]
            pv = None
            for t in range(len(blocks)):
                r = sums[p, t]
                total = r[:, :n] if c is None else r[:, :n] + c
                w = jnp.exp(logits[p, t] - total)
                if blocks[t][1]:
                    w = jnp.where(strict, w, 0.0)
                term = _dot(w, kv[p, t][1])
                pv = term if pv is None else pv + term
                c = r[:, n:] if c is None else c + r[:, n:]
            c_scr[p] = c
            pvs.append(pv)
            cmin = c if cmin is None else jnp.minimum(cmin, c)
        for p in prs:
            o_scr[p] = pvs[p] if fresh else o_scr[p] + pvs[p]
        return jnp.min(cmin)

    @pl.when(i == 0)
    def _():
        cmin_ref[0] = visit([(i, True)], True)

    @pl.when(i == 1)
    def _():
        cmin_ref[0] = visit([(i, True), (i - 1, False)], True)

    @pl.when(i >= 2)
    def _():
        cmin_ref[0] = visit([(i, True), (i - 1, False), (i - 2, False)], True)

    def cond(carry):
        j, cmin = carry
        return jnp.logical_and(j >= 0, cmin < -SB_SKIP_LOG)

    def body(carry):
        j, _ = carry
        return j - 1, visit([(j, False)], False)

    lax.while_loop(cond, body, (i - 3, cmin_ref[0]))
    for p in range(pairs):
        o = jnp.where(head_a, o_scr[p, 0:n, :], o_scr[p, n:2 * n, :])
        cols = slice(p * n, (p + 1) * n)
        o_ref[:, cols] = (o * _silu(z_ref[:, cols].astype(F32))).astype(o_ref.dtype)


def _sb_attn(proj3):
    bsz, seq, _ = proj3.shape
    n = SB_BLOCK
    w = BRANCH_WIDTH
    base = 7
    u2 = _sb_constants()
    return pl.pallas_call(
        _sb_kernel,
        out_shape=jax.ShapeDtypeStruct((bsz, seq, w), BF16),
        grid=(bsz, seq // n),
        in_specs=[
            pl.BlockSpec((None, n, w), lambda b, i: (b, i, base)),
            pl.BlockSpec((None, seq, w), lambda b, i: (b, 0, base + 1)),
            pl.BlockSpec((None, seq, w), lambda b, i: (b, 0, base + 2)),
            pl.BlockSpec((None, n, w), lambda b, i: (b, i, base + 3)),
            pl.BlockSpec(u2.shape, lambda b, i: (0, 0)),
        ],
        out_specs=pl.BlockSpec((None, n, w), lambda b, i: (b, i, 0)),
        scratch_shapes=[pltpu.VMEM((w // n, 2 * n, n), BF16),
                        pltpu.VMEM((w // n, 2 * n, n), F32),
                        pltpu.VMEM((w // n, 2 * n, n), F32),
                        pltpu.SMEM((1,), F32)],
        compiler_params=pltpu.CompilerParams(
            dimension_semantics=("arbitrary", "arbitrary"),
            vmem_limit_bytes=_mib(40)),
        name="stick_breaking",
    )(proj3, proj3, proj3, proj3, jnp.asarray(u2, BF16))


def _merge_kernel(ya_ref, yb_ref, yc_ref, g0, g1, g2, g3, g4, g5, x_ref, gate_ref, wb_ref,
                  wo_ref, fnw_ref, o_ref, *, final):
    w = BRANCH_WIDTH
    ys = (ya_ref[...], yb_ref[...], yc_ref[...])
    gl = ((g0, g1), (g2, g3), (g4, g5))
    halves = []
    for half in range(2):
        acc = None
        for nb in range(N_BRANCH):
            br = jnp.dot(ys[nb], wb_ref[nb, :, half * w:(half + 1) * w],
                         preferred_element_type=F32)
            term = jax.nn.sigmoid(gl[nb][half][...].astype(F32)) * br
            acc = term if acc is None else acc + term
        halves.append(acc.astype(BF16))
    merged = jnp.concatenate(halves, axis=1)
    out = jnp.dot(merged, wo_ref[...], preferred_element_type=F32)
    xn = x_ref[...] + gate_ref[...] * out
    if final:
        ms = jnp.mean(xn * xn, axis=-1, keepdims=True)
        xn = xn * lax.rsqrt(ms + EPS) * fnw_ref[...]
    o_ref[...] = xn


def _merge(ya, yb, yc, proj, x2, mod5, wb_bf16, wo_bf16, final_norm_w, layer, seq, final):
    m, d = x2.shape
    w = BRANCH_WIDTH
    tm = 512
    per_seq = seq // tm
    gate0 = 11
    ycol = lambda: pl.BlockSpec((tm, w), lambda i: (i, 0))
    gcol = lambda cb: pl.BlockSpec((tm, w), lambda i: (i, gate0 + cb))
    return pl.pallas_call(
        functools.partial(_merge_kernel, final=final),
        out_shape=jax.ShapeDtypeStruct((m, d), F32),
        grid=(m // tm,),
        in_specs=[
            ycol(), ycol(), ycol(),
            gcol(0), gcol(1), gcol(2), gcol(3), gcol(4), gcol(5),
            pl.BlockSpec((tm, d), lambda i: (i, 0)),
            pl.BlockSpec((None, None, None, 1, d), lambda i: (layer, i // per_seq, 2, 0, 0)),
            pl.BlockSpec((None, N_BRANCH, w, d), lambda i: (layer, 0, 0, 0)),
            pl.BlockSpec((None, d, d), lambda i: (layer, 0, 0)),
            pl.BlockSpec((1, d), lambda i: (0, 0)),
        ],
        out_specs=pl.BlockSpec((tm, d), lambda i: (i, 0)),
        compiler_params=pltpu.CompilerParams(
            dimension_semantics=("arbitrary",),
            vmem_limit_bytes=_mib(48)),
        name="merge_out",
    )(ya, yb, yc, proj, proj, proj, proj, proj, proj, x2, mod5, wb_bf16, wo_bf16,
      final_norm_w.reshape(1, d))


def kernel(x, c, ada_w, ada_b, norm_w, w_in, hgrn_lb, hgrn_norm_w, conv_w, conv_b, conv_ln_w,
           conv_ln_b, w_branch, w_out, final_norm_w):
    bsz, seq, d = x.shape
    depth = ada_w.shape[0]
    m = bsz * seq
    mod = _ada_mod(c, ada_w, ada_b)
    mod5 = mod.reshape(depth, bsz, 3, 1, d)
    w_in_b = w_in.astype(BF16)
    wb_b = w_branch.astype(BF16)
    wo_b = w_out.astype(BF16)
    x2 = x.reshape(m, d)
    for layer in range(depth):
        proj = _inproj(x2, mod5, norm_w, w_in_b, layer, seq)
        proj3 = proj.reshape(bsz, seq, proj.shape[-1])
        ya = _hgrn(proj3, hgrn_lb, hgrn_norm_w, layer)
        yb =_conv(proj3, conv_w, conv_b, conv_ln_w, conv_ln_b, layer)
        yc = _sb_attn(proj3)
        x2 = _merge(ya.reshape(m, -1), yb.reshape(m, -1), yc.reshape(m, -1), proj, x2, mod5,
                    wb_b, wo_b, final_norm_w, layer, seq, final=(layer == depth - 1))
    return x2.reshape(bsz, seq, d)
```

```python
import functools

import numpy as np
import jax
import jax.numpy as jnp
from jax import lax
from jax.experimental import pallas as pl
from jax.experimental.pallas import tpu as pltpu

F32 = jnp.float32
BF16 = jnp.bfloat16

LANES_V7X = 128
SUBLANES_V7X = 8
VMEM_BYTES_V7X = 64 * 1024 * 1024

EPS = 1e-6
TINY = 1e-30
LOG2E = 1.4426950408889634
BRANCH_WIDTH = 512
HGRN_HEADS = 4
HGRN_HEAD_DIM = BRANCH_WIDTH // HGRN_HEADS
HGRN_CHUNK = 64
HGRN_LEVELS = 6
_HGRN_MXU_LEVELS = 3
HGRN_DIRECT_MAX_LOG2 = 110.0
CONV_WIDTH = 31
CONV_HALO = 32
SB_HEADS = 8
SB_HEAD_DIM = BRANCH_WIDTH // SB_HEADS
SB_BLOCK = 128
SB_SKIP_LOG = -104.0
N_BRANCH = 3


def _mib(n):
    return int(n) * 1024 * 1024


def _silu(x):
    return x * jax.nn.sigmoid(x)


def _dot(a, b):
    return jnp.dot(a.astype(BF16), b.astype(BF16), preferred_element_type=F32)


def _dot_nt(a, b):
    return lax.dot_general(a.astype(BF16), b.astype(BF16), (((1,), (1,)), ((), ())),
                           preferred_element_type=F32)


def _dot_tn(a, b):
    return lax.dot_general(a.astype(BF16), b.astype(BF16), (((0,), (0,)), ((), ())),
                           preferred_element_type=F32)


def _split3(x):
    hi = x.astype(BF16)
    r = x - hi.astype(F32)
    mid = r.astype(BF16)
    lo = (r - mid.astype(F32)).astype(BF16)
    return hi, mid, lo


def _ada_kernel(c_ref, w_ref, b_ref, o_ref):
    c = c_ref[...]
    ca = _silu(c)
    w = w_ref[...]
    ch = ca.astype(BF16)
    cl = (ca - ch.astype(F32)).astype(BF16)
    wh = w.astype(BF16)
    wl = (w - wh.astype(F32)).astype(BF16)
    acc = (jnp.dot(ch, wh, preferred_element_type=F32)
           + jnp.dot(ch, wl, preferred_element_type=F32)
           + jnp.dot(cl, wh, preferred_element_type=F32))
    o_ref[...] = acc + b_ref[...]


def _ada_mod(c, ada_w, ada_b):
    depth, d, n = ada_w.shape
    bsz = c.shape[0]
    tn = n // 4
    return pl.pallas_call(
        _ada_kernel,
        out_shape=jax.ShapeDtypeStruct((depth, bsz, n), F32),
        grid=(depth, n // tn),
        in_specs=[
            pl.BlockSpec((bsz, d), lambda l, j: (0, 0)),
            pl.BlockSpec((None, d, tn), lambda l, j: (l, 0, j)),
            pl.BlockSpec((None, 1, tn), lambda l, j: (l, 0, j)),
        ],
        out_specs=pl.BlockSpec((None, bsz, tn), lambda l, j: (l, 0, j)),
        compiler_params=pltpu.CompilerParams(
            dimension_semantics=("arbitrary", "arbitrary"),
            vmem_limit_bytes=_mib(32)),
        name="ada_mod",
    )(c, ada_w, ada_b.reshape(depth, 1, n))


_INPROJ_NORM_ROWS = 256
_INPROJ_MM_ROWS = 1024


def _inproj_kernel(x_ref, shift_ref, scale_ref, nw_ref, w_ref, o_ref, h_scr):
    tm = x_ref.shape[0]

    @pl.when(pl.program_id(1) == 0)
    def _():
        gain = nw_ref[...] * (1.0 + scale_ref[...])
        shift = shift_ref[...]

        def body(i, carry):
            r0 = pl.multiple_of(i * _INPROJ_NORM_ROWS, _INPROJ_NORM_ROWS)
            x = x_ref[pl.ds(r0, _INPROJ_NORM_ROWS), :]
            ms = jnp.mean(x * x, axis=-1, keepdims=True)
            h = (x * lax.rsqrt(ms + EPS)) * gain + shift
            h_scr[pl.ds(r0, _INPROJ_NORM_ROWS), :] = h.astype(BF16)
            return carry

        lax.fori_loop(0, tm // _INPROJ_NORM_ROWS, body, 0)

    def mm(i, carry):
        r0 = pl.multiple_of(i * _INPROJ_MM_ROWS, _INPROJ_MM_ROWS)
        o_ref[pl.ds(r0, _INPROJ_MM_ROWS), :] = jnp.dot(
            h_scr[pl.ds(r0, _INPROJ_MM_ROWS), :], w_ref[...],
            preferred_element_type=F32).astype(o_ref.dtype)
        return carry

    lax.fori_loop(0, tm // _INPROJ_MM_ROWS, mm, 0)


def _inproj(x2, mod5, norm_w, w_in_bf16, layer, seq):
    m, d = x2.shape
    n = w_in_bf16.shape[-1]
    tm = seq
    tn = n // 4
    per_seq = seq // tm
    return pl.pallas_call(
        _inproj_kernel,
        out_shape=jax.ShapeDtypeStruct((m, n), BF16),
        grid=(m // tm, n // tn),
        in_specs=[
            pl.BlockSpec((tm, d), lambda i, j: (i, 0)),
            pl.BlockSpec((None, None, None, 1, d), lambda i, j: (layer, i // per_seq, 0, 0, 0)),
            pl.BlockSpec((None, None, None, 1, d), lambda i, j: (layer, i // per_seq, 1, 0, 0)),
            pl.BlockSpec((None, 1, d), lambda i, j: (layer, 0, 0)),
            pl.BlockSpec((None, d, tn), lambda i, j: (layer, 0, j)),
        ],
        out_specs=pl.BlockSpec((tm, tn), lambda i, j: (i, j)),
        scratch_shapes=[pltpu.VMEM((tm, d), BF16)],
        compiler_params=pltpu.CompilerParams(
            dimension_semantics=("arbitrary", "arbitrary"),
            vmem_limit_bytes=_mib(56)),
        name="inproj",
    )(x2, mod5, mod5, norm_w.reshape(norm_w.shape[0], 1, d), w_in_bf16)


def _hgrn_constants():
    L = HGRN_CHUNK
    tri = np.tril(np.ones((L, L), np.float32))
    blocks = [tri]
    r = np.arange(L)
    for l in range(_HGRN_MXU_LEVELS):
        h = 1 << l
        p = (r // (2 * h)) * (2 * h) + h - 1
        sign = np.where(r > p, 1.0, -1.0).astype(np.float32)[:, None]
        blocks.append(sign * (tri - tri[p]))
    c_all = np.concatenate(blocks, axis=0)
    c_all = np.concatenate([c_all, c_all, c_all], axis=1)
    t = r[:, None]
    s = r[None, :]
    x = t ^ s
    lev = np.where(x > 0, np.floor(np.log2(np.maximum(x, 1))).astype(np.int32), HGRN_LEVELS)
    lev = np.where(s > t, -1, lev).astype(np.int32)
    return c_all, lev


def _hgrn_kernel(q_ref, f_ref, i_ref, z_ref, lb_ref, nw_ref, c_ref, lev_ref, ind_ref, o_ref,
                 st_ref, g_scr, *, layer):
    L = HGRN_CHUNK
    K = HGRN_HEAD_DIM
    n_chunks = q_ref.shape[0] // L

    @pl.when(pl.program_id(1) == 0)
    def _():
        st_ref[...] = jnp.zeros_like(st_ref)

    lb_all = lb_ref[...]
    lb_exp = jnp.exp(lb_all - jnp.max(lb_all, axis=0, keepdims=True))
    lb_soft = lb_exp / jnp.sum(lb_exp, axis=0, keepdims=True)
    lower = jnp.zeros((1, lb_all.shape[1]), F32)
    for l in range(1, layer + 1):
        lower = lower + lb_soft[l:l + 1, :]
    one_m_lower = 1.0 - lower
    nw = nw_ref[...]

    f_all = lower + one_m_lower * jax.nn.sigmoid(f_ref[...].astype(F32))
    g_all = jnp.log(jnp.maximum(f_all, TINY)) * LOG2E
    g_scr[...] = g_all
    worst = -jnp.min(jnp.dot(ind_ref[...], g_all.astype(BF16), preferred_element_type=F32))

    def body(c, carry, direct):
        r0 = pl.multiple_of(c * L, L)
        rows = pl.ds(r0, L)
        q = _silu(q_ref[rows, :].astype(F32)) * (K ** -0.5)
        g = g_scr[rows, :]
        k = one_m_lower * jax.nn.sigmoid(-f_ref[rows, :].astype(F32))
        v = i_ref[rows, :]
        z = z_ref[rows, :].astype(F32)
        cmat = c_ref[0:L, :] if direct else c_ref[...]
        dall = jnp.dot(cmat, jnp.concatenate(_split3(g), axis=0),
                       preferred_element_type=F32)
        lev = lev_ref[...]
        heads = range(HGRN_HEADS)
        hcols = [slice(h * K, (h + 1) * K) for h in heads]

        if direct:
            pair, o_inter, upd, decay = [], [], [], []
            for h in heads:
                cols = hcols[h]
                qh, kh, vh = q[:, cols], k[:, cols], v[:, cols]
                b = dall[:, cols]
                b_last = b[L - 1:L, :]
                qe = qh * jnp.exp2(b)
                pair.append(_dot_nt(qe, kh * jnp.exp2(-b)))
                o_inter.append(_dot_nt(qe, st_ref[h]))
                upd.append(_dot_tn(vh, kh * jnp.exp2(b_last - b)))
                decay.append(jnp.exp2(b_last))
            outs = [o_inter[h] + _dot(jnp.where(lev >= 0, pair[h], 0.0), v[:, hcols[h]])
                    for h in heads]
            for h in heads:
                st_ref[h] = st_ref[h] * decay[h] + upd[h]
                o = outs[h]
                on = o * lax.rsqrt(jnp.mean(o * o, axis=-1, keepdims=True) + EPS) * nw
                o_ref[rows, hcols[h]] = (on * _silu(z[:, hcols[h]])).astype(o_ref.dtype)
            return carry

        def boundary_gap(b, l, cols):
            if l < _HGRN_MXU_LEVELS:
                return dall[(l + 1) * L:(l + 2) * L, cols]
            hw = 1 << l
            parts = []
            for s in range(0, L, 2 * hw):
                bp = b[s + hw - 1:s + hw, :]
                parts += [bp - b[s:s + hw, :], b[s + hw:s + 2 * hw, :] - bp]
            return jnp.concatenate(parts, axis=0)

        pair, o_inter, upd, decay = [], [], [], []
        for h in heads:
            cols = hcols[h]
            qh, kh, vh = q[:, cols], k[:, cols], v[:, cols]
            b = dall[0:L, cols]
            b_last = b[L - 1:L, :]
            o_inter.append(_dot_nt(qh * jnp.exp2(b), st_ref[h]))
            upd.append(_dot_tn(vh, kh * jnp.exp2(b_last - b)))
            decay.append(jnp.exp2(b_last))
            prods = [_dot_nt(qh, kh)]
            for l in range(HGRN_LEVELS):
                fac = jnp.exp2(boundary_gap(b, l, cols))
                prods.append(_dot_nt(qh * fac, kh * fac))
            pair.append(prods)
        outs = []
        for h in heads:
            a = jnp.where(lev == HGRN_LEVELS, pair[h][0], 0.0)
            for l in range(HGRN_LEVELS):
                a = jnp.where(lev == l, pair[h][l + 1], a)
            outs.append(o_inter[h] + _dot(a, v[:, hcols[h]]))
        for h in heads:
            st_ref[h] = st_ref[h] * decay[h] + upd[h]
            o = outs[h]
            on = o * lax.rsqrt(jnp.mean(o * o, axis=-1, keepdims=True) + EPS) * nw
            o_ref[rows, hcols[h]] = (on * _silu(z[:, hcols[h]])).astype(o_ref.dtype)
        return carry

    direct_ok = worst < HGRN_DIRECT_MAX_LOG2

    @pl.when(direct_ok)
    def _():
        lax.fori_loop(0, n_chunks, functools.partial(body, direct=True), 0, unroll=8)

    @pl.when(jnp.logical_not(direct_ok))
    def _():
        lax.fori_loop(0, n_chunks, functools.partial(body, direct=False), 0, unroll=4)


def _hgrn(proj3, hgrn_lb, hgrn_norm_w, layer):
    bsz, seq, _ = proj3.shape
    w = BRANCH_WIDTH
    tb = 512
    c_all, lev = _hgrn_constants()
    ind = np.zeros((2 * SUBLANES_V7X, tb), np.float32)
    for c in range(tb // HGRN_CHUNK):
        ind[c, c * HGRN_CHUNK:(c + 1) * HGRN_CHUNK] = 1.0
    col = lambda cb: pl.BlockSpec((None, tb, w), lambda b, s: (b, s, cb))
    return pl.pallas_call(
        functools.partial(_hgrn_kernel, layer=layer),
        out_shape=jax.ShapeDtypeStruct((bsz, seq, w), BF16),
        grid=(bsz, seq // tb),
        in_specs=[
            col(0), col(1), col(2), col(3),
            pl.BlockSpec(hgrn_lb.shape, lambda b, s: (0, 0)),
            pl.BlockSpec((None, 1, HGRN_HEAD_DIM), lambda b, s: (layer, 0, 0)),
            pl.BlockSpec(c_all.shape, lambda b, s: (0, 0)),
            pl.BlockSpec(lev.shape, lambda b, s: (0, 0)),
            pl.BlockSpec(ind.shape, lambda b, s: (0, 0)),
        ],
        out_specs=pl.BlockSpec((None, tb, w), lambda b, s: (b, s, 0)),
        scratch_shapes=[pltpu.VMEM((HGRN_HEADS, HGRN_HEAD_DIM, HGRN_HEAD_DIM), F32),
                        pltpu.VMEM((tb, w), F32)],
        compiler_params=pltpu.CompilerParams(
            dimension_semantics=("arbitrary", "arbitrary"),
            vmem_limit_bytes=_mib(32)),
        name="hgrn2",
    )(proj3, proj3, proj3, proj3, hgrn_lb,
      hgrn_norm_w.reshape(hgrn_norm_w.shape[0], 1, HGRN_HEAD_DIM),
      jnp.asarray(c_all, BF16), jnp.asarray(lev), jnp.asarray(ind, BF16))


_CONV_ROWS = 32
_CONV_GLU_ROWS = 64


def _conv_kernel(a_ref, g_ref, z_ref, w_ref, b_ref, lnw_ref, lnb_ref, o_ref, u_scr, wb_scr):
    ts = a_ref.shape[0]
    sub = SUBLANES_V7X

    @pl.when(pl.program_id(1) == 0)
    def _():
        u_scr[:, 0:CONV_HALO, :] = jnp.zeros((sub, CONV_HALO, u_scr.shape[2]), F32)

    @pl.when(pl.program_id(1) > 0)
    def _():
        for res in range(sub):
            u_scr[res, 0:CONV_HALO - res, :] = u_scr[res, ts:ts + CONV_HALO - res, :]

    for c in range(ts // _CONV_GLU_ROWS):
        r0 = c * _CONV_GLU_ROWS
        u = (a_ref[r0:r0 + _CONV_GLU_ROWS, :].astype(F32)
             * jax.nn.sigmoid(g_ref[r0:r0 + _CONV_GLU_ROWS, :].astype(F32)))
        for res in range(sub):
            u_scr[res, CONV_HALO - res + r0:CONV_HALO - res + r0 + _CONV_GLU_ROWS, :] = u

    wd = w_ref.shape[1]
    for j in range(CONV_WIDTH):
        wb_scr[j] = jnp.broadcast_to(w_ref[j:j + 1, :], (sub, wd))
    wb_scr[CONV_WIDTH] = jnp.broadcast_to(b_ref[...], (sub, wd))
    lnw = lnw_ref[...]
    lnb = lnb_ref[...]
    first = CONV_HALO - (CONV_WIDTH - 1)
    tiles = _CONV_ROWS // sub

    def chunk(c, carry):
        r0 = pl.multiple_of(c * _CONV_ROWS, _CONV_ROWS)
        acc = jnp.broadcast_to(wb_scr[CONV_WIDTH][None], (tiles, sub, wd))
        for j in range(CONV_WIDTH):
            res = (first + j) % sub
            win = u_scr[res, pl.ds(r0 + (first + j - res), _CONV_ROWS), :]
            acc = acc + wb_scr[j][None] * win.reshape(tiles, sub, wd)
        acc = acc.reshape(_CONV_ROWS, wd)
        mu = jnp.mean(acc, axis=-1, keepdims=True)
        d = acc - mu
        var = jnp.mean(d * d, axis=-1, keepdims=True)
        y = _silu(d * lax.rsqrt(var + EPS) * lnw + lnb)
        z = z_ref[pl.ds(r0, _CONV_ROWS), :].astype(F32)
        o_ref[pl.ds(r0, _CONV_ROWS), :] = (y * _silu(z)).astype(o_ref.dtype)
        return carry

    lax.fori_loop(0, ts // _CONV_ROWS, chunk, 0, unroll=4)


def _conv(proj3, conv_w, conv_b, ln_w, ln_b, layer):
    bsz, seq, _ = proj3.shape
    w = BRANCH_WIDTH
    ts = 256
    col = lambda cb: pl.BlockSpec((None, ts, w), lambda b, s: (b, s, cb))
    vec = lambda: pl.BlockSpec((None, 1, w), lambda b, s: (layer, 0, 0))
    depth = conv_w.shape[0]
    return pl.pallas_call(
        _conv_kernel,
        out_shape=jax.ShapeDtypeStruct((bsz, seq, w), BF16),
        grid=(bsz, seq // ts),
        in_specs=[
            col(4), col(5), col(6),
            pl.BlockSpec((None, CONV_WIDTH, w), lambda b, s: (layer, 0, 0)),
            vec(), vec(), vec(),
        ],
        out_specs=pl.BlockSpec((None, ts, w), lambda b, s: (b, s, 0)),
        scratch_shapes=[pltpu.VMEM((SUBLANES_V7X, CONV_HALO + ts, w), F32),
                        pltpu.VMEM((CONV_WIDTH + 1, SUBLANES_V7X, w), F32)],
        compiler_params=pltpu.CompilerParams(
            dimension_semantics=("arbitrary", "arbitrary"),
            vmem_limit_bytes=_mib(32)),
        name="conv_module",
    )(proj3, proj3, proj3, conv_w, conv_b.reshape(depth, 1, w), ln_w.reshape(depth, 1, w),
      ln_b.reshape(depth, 1, w))


def _sb_constants():
    n = SB_BLOCK
    u = (np.arange(n)[:, None] >= np.arange(n)[None, :]).astype(np.float32)
    return np.concatenate([u, np.ones((n, n), np.float32)], axis=1)


def _sb_kernel(q_ref, k_ref, v_ref, z_ref, u_ref, o_ref, q_scr, c_scr, o_scr, cmin_ref):
    n = SB_BLOCK
    pairs = q_ref.shape[1] // n
    prs = range(pairs)
    i = pl.program_id(1)
    lane = lax.broadcasted_iota(jnp.int32, (2 * n, n), 1)
    row = lax.broadcasted_iota(jnp.int32, (2 * n, n), 0)
    strict = lane < (row & (n - 1))
    head_a = lax.broadcasted_iota(jnp.int32, (n, n), 1) < SB_HEAD_DIM

    for p in prs:
        q = q_ref[:, p * n:(p + 1) * n].astype(F32) * (SB_HEAD_DIM ** -0.5)
        q_scr[p, 0:n, :] = jnp.where(head_a, q, 0.0).astype(BF16)
        q_scr[p, n:2 * n, :] = jnp.where(head_a, 0.0, q).astype(BF16)

    def visit(blocks, fresh):
        units = [(p, t) for p in prs for t in range(len(blocks))]
        kv = {}
        for t, (j, _) in enumerate(blocks):
            rows = pl.ds(pl.multiple_of(j * n, n), n)
            for p in prs:
                kv[p, t] = (k_ref[rows, p * n:(p + 1) * n], v_ref[rows, p * n:(p + 1) * n])
        logits = {u: _dot_nt(q_scr[u[0]], kv[u][0]) for u in units}
        sums = {}
        for u in units:
            p, t = u
            x = logits[u]
            drop = jnp.maximum(x, 0.0) + jnp.log(1.0 + jnp.exp2(jnp.abs(x) * (-LOG2E)))
            if blocks[t][1]:
                drop = jnp.where(strict, drop, 0.0)
            sums[u] = _dot(drop, u_ref[...])
        cmin = None
        pvs = []
        for p in prs:
            c = None if fresh else c_scr[p]
            pv = None
            for t in range(len(blocks)):
                r = sums[p, t]
                total = r[:, :n] if c is None else r[:, :n] + c
                w = jnp.exp(logits[p, t] - total)
                if blocks[t][1]:
                    w = jnp.where(strict, w, 0.0)
                term = _dot(w, kv[p, t][1])
                pv = term if pv is None else pv + term
                c = r[:, n:] if c is None else c + r[:, n:]
            c_scr[p] = c
            pvs.append(pv)
            cmin = c if cmin is None else jnp.minimum(cmin, c)
        for p in prs:
            o_scr[p] = pvs[p] if fresh else o_scr[p] + pvs[p]
        return jnp.min(cmin)

    @pl.when(i == 0)
    def _():
        cmin_ref[0] = visit([(i, True)], True)

    @pl.when(i == 1)
    def _():
        cmin_ref[0] = visit([(i, True), (i - 1, False)], True)

    @pl.when(i >= 2)
    def _():
        cmin_ref[0] = visit([(i, True), (i - 1, False), (i - 2, False)], True)

    def cond(carry):
        j, cmin = carry
        return jnp.logical_and(j >= 0, cmin < -SB_SKIP_LOG)

    def body(carry):
        j, _ = carry
        return j - 1, visit([(j, False)], False)

    lax.while_loop(cond, body, (i - 3, cmin_ref[0]))
    for p in range(pairs):
        o = jnp.where(head_a, o_scr[p, 0:n, :], o_scr[p, n:2 * n, :])
        cols = slice(p * n, (p + 1) * n)
        o_ref[:, cols] = (o * _silu(z_ref[:, cols].astype(F32))).astype(o_ref.dtype)


def _sb_attn(proj3):
    bsz, seq, _ = proj3.shape
    n = SB_BLOCK
    w = BRANCH_WIDTH
    base = 7
    u2 = _sb_constants()
    return pl.pallas_call(
        _sb_kernel,
        out_shape=jax.ShapeDtypeStruct((bsz, seq, w), BF16),
        grid=(bsz, seq // n),
        in_specs=[
            pl.BlockSpec((None, n, w), lambda b, i: (b, i, base)),
            pl.BlockSpec((None, seq, w), lambda b, i: (b, 0, base + 1)),
            pl.BlockSpec((None, seq, w), lambda b, i: (b, 0, base + 2)),
            pl.BlockSpec((None, n, w), lambda b, i: (b, i, base + 3)),
            pl.BlockSpec(u2.shape, lambda b, i: (0, 0)),
        ],
        out_specs=pl.BlockSpec((None, n, w), lambda b, i: (b, i, 0)),
        scratch_shapes=[pltpu.VMEM((w // n, 2 * n, n), BF16),
                        pltpu.VMEM((w // n, 2 * n, n), F32),
                        pltpu.VMEM((w // n, 2 * n, n), F32),
                        pltpu.SMEM((1,), F32)],
        compiler_params=pltpu.CompilerParams(
            dimension_semantics=("arbitrary", "arbitrary"),
            vmem_limit_bytes=_mib(40)),
        name="stick_breaking",
    )(proj3, proj3, proj3, proj3, jnp.asarray(u2, BF16))


def _merge_kernel(ya_ref, yb_ref, yc_ref, g0, g1, g2, g3, g4, g5, x_ref, gate_ref, wb_ref,
                  wo_ref, fnw_ref, o_ref, *, final):
    w = BRANCH_WIDTH
    ys = (ya_ref[...], yb_ref[...], yc_ref[...])
    gl = ((g0, g1), (g2, g3), (g4, g5))
    halves = []
    for half in range(2):
        acc = None
        for nb in range(N_BRANCH):
            br = jnp.dot(ys[nb], wb_ref[nb, :, half * w:(half + 1) * w],
                         preferred_element_type=F32)
            term = jax.nn.sigmoid(gl[nb][half][...].astype(F32)) * br
            acc = term if acc is None else acc + term
        halves.append(acc.astype(BF16))
    merged = jnp.concatenate(halves, axis=1)
    out = jnp.dot(merged, wo_ref[...], preferred_element_type=F32)
    xn = x_ref[...] + gate_ref[...] * out
    if final:
        ms = jnp.mean(xn * xn, axis=-1, keepdims=True)
        xn = xn * lax.rsqrt(ms + EPS) * fnw_ref[...]
    o_ref[...] = xn


def _merge(ya, yb, yc, proj, x2, mod5, wb_bf16, wo_bf16, final_norm_w, layer, seq, final):
    m, d = x2.shape
    w = BRANCH_WIDTH
    tm = 512
    per_seq = seq // tm
    gate0 = 11
    ycol = lambda: pl.BlockSpec((tm, w), lambda i: (i, 0))
    gcol = lambda cb: pl.BlockSpec((tm, w), lambda i: (i, gate0 + cb))
    return pl.pallas_call(
        functools.partial(_merge_kernel, final=final),
        out_shape=jax.ShapeDtypeStruct((m, d), F32),
        grid=(m // tm,),
        in_specs=[
            ycol(), ycol(), ycol(),
            gcol(0), gcol(1), gcol(2), gcol(3), gcol(4), gcol(5),
            pl.BlockSpec((tm, d), lambda i: (i, 0)),
            pl.BlockSpec((None, None, None, 1, d), lambda i: (layer, i // per_seq, 2, 0, 0)),
            pl.BlockSpec((None, N_BRANCH, w, d), lambda i: (layer, 0, 0, 0)),
            pl.BlockSpec((None, d, d), lambda i: (layer, 0, 0)),
            pl.BlockSpec((1, d), lambda i: (0, 0)),
        ],
        out_specs=pl.BlockSpec((tm, d), lambda i: (i, 0)),
        compiler_params=pltpu.CompilerParams(
            dimension_semantics=("arbitrary",),
            vmem_limit_bytes=_mib(48)),
        name="merge_out",
    )(ya, yb, yc, proj, proj, proj, proj, proj, proj, x2, mod5, wb_bf16, wo_bf16,
      final_norm_w.reshape(1, d))


def kernel(x, c, ada_w, ada_b, norm_w, w_in, hgrn_lb, hgrn_norm_w, conv_w, conv_b, conv_ln_w,
           conv_ln_b, w_branch, w_out, final_norm_w):
    bsz, seq, d = x.shape
    depth = ada_w.shape[0]
    m = bsz * seq
    mod = _ada_mod(c, ada_w, ada_b)
    mod5 = mod.reshape(depth, bsz, 3, 1, d)
    w_in_b = w_in.astype(BF16)
    wb_b = w_branch.astype(BF16)
    wo_b = w_out.astype(BF16)
    x2 = x.reshape(m, d)
    for layer in range(depth):
        proj = _inproj(x2, mod5, norm_w, w_in_b, layer, seq)
        proj3 = proj.reshape(bsz, seq, proj.shape[-1])
        ya = _hgrn(proj3, hgrn_lb, hgrn_norm_w, layer)
        yb =_conv(proj3, conv_w, conv_b, conv_ln_w, conv_ln_b, layer)
        yc = _sb_attn(proj3)
        x2 = _merge(ya.reshape(m, -1), yb.reshape(m, -1), yc.reshape(m, -1), proj, x2, mod5,
                    wb_b, wo_b, final_norm_w, layer, seq, final=(layer == depth - 1))
    return x2.reshape(bsz, seq, d)
```

```python
import functools

import numpy as np
import jax
import jax.numpy as jnp
from jax import lax
from jax.experimental import pallas as pl
from jax.experimental.pallas import tpu as pltpu

F32 = jnp.float32
BF16 = jnp.bfloat16

LANES_V7X = 128
SUBLANES_V7X = 8
VMEM_BYTES_V7X = 64 * 1024 * 1024

EPS = 1e-6
TINY = 1e-30
LOG2E = 1.4426950408889634
BRANCH_WIDTH = 512
HGRN_HEADS = 4
HGRN_HEAD_DIM = BRANCH_WIDTH // HGRN_HEADS
HGRN_CHUNK = 64
HGRN_LEVELS = 6
_HGRN_MXU_LEVELS = 3
HGRN_DIRECT_MAX_LOG2 = 120.0
CONV_WIDTH = 31
CONV_HALO = 32
SB_HEADS = 8
SB_HEAD_DIM = BRANCH_WIDTH // SB_HEADS
SB_BLOCK = 128
SB_SKIP_LOG = -104.0
N_BRANCH = 3


def _mib(n):
    return int(n) * 1024 * 1024


def _silu(x):
    return x * jax.nn.sigmoid(x)


def _dot(a, b):
    return jnp.dot(a.astype(BF16), b.astype(BF16), preferred_element_type=F32)


def _dot_nt(a, b):
    return lax.dot_general(a.astype(BF16), b.astype(BF16), (((1,), (1,)), ((), ())),
                           preferred_element_type=F32)


def _dot_tn(a, b):
    return lax.dot_general(a.astype(BF16), b.astype(BF16), (((0,), (0,)), ((), ())),
                           preferred_element_type=F32)


def _split3(x):
    hi = x.astype(BF16)
    r = x - hi.astype(F32)
    mid = r.astype(BF16)
    lo = (r - mid.astype(F32)).astype(BF16)
    return hi, mid, lo


def _ada_kernel(c_ref, w_ref, b_ref, o_ref):
    c = c_ref[...]
    ca = _silu(c)
    w = w_ref[...]
    ch = ca.astype(BF16)
    cl = (ca - ch.astype(F32)).astype(BF16)
    wh = w.astype(BF16)
    wl = (w - wh.astype(F32)).astype(BF16)
    acc = (jnp.dot(ch, wh, preferred_element_type=F32)
           + jnp.dot(ch, wl, preferred_element_type=F32)
           + jnp.dot(cl, wh, preferred_element_type=F32))
    o_ref[...] = acc + b_ref[...]


def _ada_mod(c, ada_w, ada_b):
    depth, d, n = ada_w.shape
    bsz = c.shape[0]
    tn = n // 4
    return pl.pallas_call(
        _ada_kernel,
        out_shape=jax.ShapeDtypeStruct((depth, bsz, n), F32),
        grid=(depth, n // tn),
        in_specs=[
            pl.BlockSpec((bsz, d), lambda l, j: (0, 0)),
            pl.BlockSpec((None, d, tn), lambda l, j: (l, 0, j)),
            pl.BlockSpec((None, 1, tn), lambda l, j: (l, 0, j)),
        ],
        out_specs=pl.BlockSpec((None, bsz, tn), lambda l, j: (l, 0, j)),
        compiler_params=pltpu.CompilerParams(
            dimension_semantics=("arbitrary", "arbitrary"),
            vmem_limit_bytes=_mib(32)),
        name="ada_mod",
    )(c, ada_w, ada_b.reshape(depth, 1, n))


_INPROJ_NORM_ROWS = 256
_INPROJ_MM_ROWS = 1024


def _inproj_kernel(x_ref, shift_ref, scale_ref, nw_ref, w_ref, o_ref, h_scr):
    tm = x_ref.shape[0]

    @pl.when(pl.program_id(1) == 0)
    def _():
        gain = nw_ref[...] * (1.0 + scale_ref[...])
        shift = shift_ref[...]

        def body(i, carry):
            r0 = pl.multiple_of(i * _INPROJ_NORM_ROWS, _INPROJ_NORM_ROWS)
            x = x_ref[pl.ds(r0, _INPROJ_NORM_ROWS), :]
            ms = jnp.mean(x * x, axis=-1, keepdims=True)
            h = (x * lax.rsqrt(ms + EPS)) * gain + shift
            h_scr[pl.ds(r0, _INPROJ_NORM_ROWS), :] = h.astype(BF16)
            return carry

        lax.fori_loop(0, tm // _INPROJ_NORM_ROWS, body, 0)

    def mm(i, carry):
        r0 = pl.multiple_of(i * _INPROJ_MM_ROWS, _INPROJ_MM_ROWS)
        o_ref[pl.ds(r0, _INPROJ_MM_ROWS), :] = jnp.dot(
            h_scr[pl.ds(r0, _INPROJ_MM_ROWS), :], w_ref[...],
            preferred_element_type=F32).astype(o_ref.dtype)
        return carry

    lax.fori_loop(0, tm // _INPROJ_MM_ROWS, mm, 0)


def _inproj(x2, mod5, norm_w, w_in_bf16, layer, seq):
    m, d = x2.shape
    n = w_in_bf16.shape[-1]
    tm = seq
    tn = n // 4
    per_seq = seq // tm
    return pl.pallas_call(
        _inproj_kernel,
        out_shape=jax.ShapeDtypeStruct((m, n), BF16),
        grid=(m // tm, n // tn),
        in_specs=[
            pl.BlockSpec((tm, d), lambda i, j: (i, 0)),
            pl.BlockSpec((None, None, None, 1, d), lambda i, j: (layer, i // per_seq, 0, 0, 0)),
            pl.BlockSpec((None, None, None, 1, d), lambda i, j: (layer, i // per_seq, 1, 0, 0)),
            pl.BlockSpec((None, 1, d), lambda i, j: (layer, 0, 0)),
            pl.BlockSpec((None, d, tn), lambda i, j: (layer, 0, j)),
        ],
        out_specs=pl.BlockSpec((tm, tn), lambda i, j: (i, j)),
        scratch_shapes=[pltpu.VMEM((tm, d), BF16)],
        compiler_params=pltpu.CompilerParams(
            dimension_semantics=("arbitrary", "arbitrary"),
            vmem_limit_bytes=_mib(56)),
        name="inproj",
    )(x2, mod5, mod5, norm_w.reshape(norm_w.shape[0], 1, d), w_in_bf16)


def _hgrn_constants():
    L = HGRN_CHUNK
    tri = np.tril(np.ones((L, L), np.float32))
    blocks = [tri]
    r = np.arange(L)
    for l in range(_HGRN_MXU_LEVELS):
        h = 1 << l
        p = (r // (2 * h)) * (2 * h) + h - 1
        sign = np.where(r > p, 1.0, -1.0).astype(np.float32)[:, None]
        blocks.append(sign * (tri - tri[p]))
    c_all = np.concatenate(blocks, axis=0)
    c_all = np.concatenate([c_all, c_all, c_all], axis=1)
    t = r[:, None]
    s = r[None, :]
    x = t ^ s
    lev = np.where(x > 0, np.floor(np.log2(np.maximum(x, 1))).astype(np.int32), HGRN_LEVELS)
    lev = np.where(s > t, -1, lev).astype(np.int32)
    return c_all, lev


def _hgrn_kernel(q_ref, f_ref, i_ref, z_ref, lb_ref, nw_ref, c_ref, lev_ref, ind_ref, o_ref,
                 st_ref, g_scr, *, layer):
    L = HGRN_CHUNK
    K = HGRN_HEAD_DIM
    n_chunks = q_ref.shape[0] // L

    @pl.when(pl.program_id(1) == 0)
    def _():
        st_ref[...] = jnp.zeros_like(st_ref)

    lb_all = lb_ref[...]
    lb_exp = jnp.exp(lb_all - jnp.max(lb_all, axis=0, keepdims=True))
    lb_soft = lb_exp / jnp.sum(lb_exp, axis=0, keepdims=True)
    lower = jnp.zeros((1, lb_all.shape[1]), F32)
    for l in range(1, layer + 1):
        lower = lower + lb_soft[l:l + 1, :]
    one_m_lower = 1.0 - lower
    nw = nw_ref[...]

    f_all = lower + one_m_lower * jax.nn.sigmoid(f_ref[...].astype(F32))
    g_all = jnp.log(jnp.maximum(f_all, TINY)) * LOG2E
    g_scr[...] = g_all
    worst = -jnp.min(jnp.dot(ind_ref[...], g_all.astype(BF16), preferred_element_type=F32))

    def body(c, carry, direct):
        r0 = pl.multiple_of(c * L, L)
        rows = pl.ds(r0, L)
        q = _silu(q_ref[rows, :].astype(F32)) * (K ** -0.5)
        g = g_scr[rows, :]
        k = one_m_lower * jax.nn.sigmoid(-f_ref[rows, :].astype(F32))
        v = i_ref[rows, :]
        z = z_ref[rows, :].astype(F32)
        cmat = c_ref[0:L, :] if direct else c_ref[...]
        dall = jnp.dot(cmat, jnp.concatenate(_split3(g), axis=0),
                       preferred_element_type=F32)
        lev = lev_ref[...]
        heads = range(HGRN_HEADS)
        hcols = [slice(h * K, (h + 1) * K) for h in heads]

        if direct:
            pair, o_inter, upd, decay = [], [], [], []
            for h in heads:
                cols = hcols[h]
                qh, kh, vh = q[:, cols], k[:, cols], v[:, cols]
                b = dall[:, cols]
                b_last = b[L - 1:L, :]
                qe = qh * jnp.exp2(b)
                pair.append(_dot_nt(qe, kh * jnp.exp2(-b)))
                o_inter.append(_dot_nt(qe, st_ref[h]))
                upd.append(_dot_tn(vh, kh * jnp.exp2(b_last - b)))
                decay.append(jnp.exp2(b_last))
            outs = [o_inter[h] + _dot(jnp.where(lev >= 0, pair[h], 0.0), v[:, hcols[h]])
                    for h in heads]
            for h in heads:
                st_ref[h] = st_ref[h] * decay[h] + upd[h]
                o = outs[h]
                on = o * lax.rsqrt(jnp.mean(o * o, axis=-1, keepdims=True) + EPS) * nw
                o_ref[rows, hcols[h]] = (on * _silu(z[:, hcols[h]])).astype(o_ref.dtype)
            return carry

        def boundary_gap(b, l, cols):
            if l < _HGRN_MXU_LEVELS:
                return dall[(l + 1) * L:(l + 2) * L, cols]
            hw = 1 << l
            parts = []
            for s in range(0, L, 2 * hw):
                bp = b[s + hw - 1:s + hw, :]
                parts += [bp - b[s:s + hw, :], b[s + hw:s + 2 * hw, :] - bp]
            return jnp.concatenate(parts, axis=0)

        pair, o_inter, upd, decay = [], [], [], []
        for h in heads:
            cols = hcols[h]
            qh, kh, vh = q[:, cols], k[:, cols], v[:, cols]
            b = dall[0:L, cols]
            b_last = b[L - 1:L, :]
            o_inter.append(_dot_nt(qh * jnp.exp2(b), st_ref[h]))
            upd.append(_dot_tn(vh, kh * jnp.exp2(b_last - b)))
            decay.append(jnp.exp2(b_last))
            prods = [_dot_nt(qh, kh)]
            for l in range(HGRN_LEVELS):
                fac = jnp.exp2(boundary_gap(b, l, cols))
                prods.append(_dot_nt(qh * fac, kh * fac))
            pair.append(prods)
        outs = []
        for h in heads:
            a = jnp.where(lev == HGRN_LEVELS, pair[h][0], 0.0)
            for l in range(HGRN_LEVELS):
                a = jnp.where(lev == l, pair[h][l + 1], a)
            outs.append(o_inter[h] + _dot(a, v[:, hcols[h]]))
        for h in heads:
            st_ref[h] = st_ref[h] * decay[h] + upd[h]
            o = outs[h]
            on = o * lax.rsqrt(jnp.mean(o * o, axis=-1, keepdims=True) + EPS) * nw
            o_ref[rows, hcols[h]] = (on * _silu(z[:, hcols[h]])).astype(o_ref.dtype)
        return carry

    direct_ok = worst < HGRN_DIRECT_MAX_LOG2

    @pl.when(direct_ok)
    def _():
        lax.fori_loop(0, n_chunks, functools.partial(body, direct=True), 0, unroll=8)

    @pl.when(jnp.logical_not(direct_ok))
    def _():
        lax.fori_loop(0, n_chunks, functools.partial(body, direct=False), 0, unroll=4)


def _hgrn(proj3, hgrn_lb, hgrn_norm_w, layer):
    bsz, seq, _ = proj3.shape
    w = BRANCH_WIDTH
    tb = 512
    c_all, lev = _hgrn_constants()
    ind = np.zeros((2 * SUBLANES_V7X, tb), np.float32)
    for c in range(tb // HGRN_CHUNK):
        ind[c, c * HGRN_CHUNK:(c + 1) * HGRN_CHUNK] = 1.0
    col = lambda cb: pl.BlockSpec((None, tb, w), lambda b, s: (b, s, cb))
    return pl.pallas_call(
        functools.partial(_hgrn_kernel, layer=layer),
        out_shape=jax.ShapeDtypeStruct((bsz, seq, w), BF16),
        grid=(bsz, seq // tb),
        in_specs=[
            col(0), col(1), col(2), col(3),
            pl.BlockSpec(hgrn_lb.shape, lambda b, s: (0, 0)),
            pl.BlockSpec((None, 1, HGRN_HEAD_DIM), lambda b, s: (layer, 0, 0)),
            pl.BlockSpec(c_all.shape, lambda b, s: (0, 0)),
            pl.BlockSpec(lev.shape, lambda b, s: (0, 0)),
            pl.BlockSpec(ind.shape, lambda b, s: (0, 0)),
        ],
        out_specs=pl.BlockSpec((None, tb, w), lambda b, s: (b, s, 0)),
        scratch_shapes=[pltpu.VMEM((HGRN_HEADS, HGRN_HEAD_DIM, HGRN_HEAD_DIM), F32),
                        pltpu.VMEM((tb, w), F32)],
        compiler_params=pltpu.CompilerParams(
            dimension_semantics=("arbitrary", "arbitrary"),
            vmem_limit_bytes=_mib(32)),
        name="hgrn2",
    )(proj3, proj3, proj3, proj3, hgrn_lb,
      hgrn_norm_w.reshape(hgrn_norm_w.shape[0], 1, HGRN_HEAD_DIM),
      jnp.asarray(c_all, BF16), jnp.asarray(lev), jnp.asarray(ind, BF16))


_CONV_ROWS = 32
_CONV_GLU_ROWS = 64


def _conv_kernel(a_ref, g_ref, z_ref, w_ref, b_ref, lnw_ref, lnb_ref, o_ref, u_scr, wb_scr):
    ts = a_ref.shape[0]
    sub = SUBLANES_V7X

    @pl.when(pl.program_id(1) == 0)
    def _():
        u_scr[:, 0:CONV_HALO, :] = jnp.zeros((sub, CONV_HALO, u_scr.shape[2]), F32)

    @pl.when(pl.program_id(1) > 0)
    def _():
        for res in range(sub):
            u_scr[res, 0:CONV_HALO - res, :] = u_scr[res, ts:ts + CONV_HALO - res, :]

    for c in range(ts // _CONV_GLU_ROWS):
        r0 = c * _CONV_GLU_ROWS
        u = (a_ref[r0:r0 + _CONV_GLU_ROWS, :].astype(F32)
             * jax.nn.sigmoid(g_ref[r0:r0 + _CONV_GLU_ROWS, :].astype(F32)))
        for res in range(sub):
            u_scr[res, CONV_HALO - res + r0:CONV_HALO - res + r0 + _CONV_GLU_ROWS, :] = u

    wd = w_ref.shape[1]
    for j in range(CONV_WIDTH):
        wb_scr[j] = jnp.broadcast_to(w_ref[j:j + 1, :], (sub, wd))
    wb_scr[CONV_WIDTH] = jnp.broadcast_to(b_ref[...], (sub, wd))
    lnw = lnw_ref[...]
    lnb = lnb_ref[...]
    first = CONV_HALO - (CONV_WIDTH - 1)
    tiles = _CONV_ROWS // sub

    def chunk(c, carry):
        r0 = pl.multiple_of(c * _CONV_ROWS, _CONV_ROWS)
        acc = jnp.broadcast_to(wb_scr[CONV_WIDTH][None], (tiles, sub, wd))
        for j in range(CONV_WIDTH):
            res = (first + j) % sub
            win = u_scr[res, pl.ds(r0 + (first + j - res), _CONV_ROWS), :]
            acc = acc + wb_scr[j][None] * win.reshape(tiles, sub, wd)
        acc = acc.reshape(_CONV_ROWS, wd)
        mu = jnp.mean(acc, axis=-1, keepdims=True)
        d = acc - mu
        var = jnp.mean(d * d, axis=-1, keepdims=True)
        y = _silu(d * lax.rsqrt(var + EPS) * lnw + lnb)
        z = z_ref[pl.ds(r0, _CONV_ROWS), :].astype(F32)
        o_ref[pl.ds(r0, _CONV_ROWS), :] = (y * _silu(z)).astype(o_ref.dtype)
        return carry

    lax.fori_loop(0, ts // _CONV_ROWS, chunk, 0, unroll=4)


def _conv(proj3, conv_w, conv_b, ln_w, ln_b, layer):
    bsz, seq, _ = proj3.shape
    w = BRANCH_WIDTH
    ts = 256
    col = lambda cb: pl.BlockSpec((None, ts, w), lambda b, s: (b, s, cb))
    vec = lambda: pl.BlockSpec((None, 1, w), lambda b, s: (layer, 0, 0))
    depth = conv_w.shape[0]
    return pl.pallas_call(
        _conv_kernel,
        out_shape=jax.ShapeDtypeStruct((bsz, seq, w), BF16),
        grid=(bsz, seq // ts),
        in_specs=[
            col(4), col(5), col(6),
            pl.BlockSpec((None, CONV_WIDTH, w), lambda b, s: (layer, 0, 0)),
            vec(), vec(), vec(),
        ],
        out_specs=pl.BlockSpec((None, ts, w), lambda b, s: (b, s, 0)),
        scratch_shapes=[pltpu.VMEM((SUBLANES_V7X, CONV_HALO + ts, w), F32),
                        pltpu.VMEM((CONV_WIDTH + 1, SUBLANES_V7X, w), F32)],
        compiler_params=pltpu.CompilerParams(
            dimension_semantics=("arbitrary", "arbitrary"),
            vmem_limit_bytes=_mib(32)),
        name="conv_module",
    )(proj3, proj3, proj3, conv_w, conv_b.reshape(depth, 1, w), ln_w.reshape(depth, 1, w),
      ln_b.reshape(depth, 1, w))


def _sb_constants():
    n = SB_BLOCK
    u = (np.arange(n)[:, None] >= np.arange(n)[None, :]).astype(np.float32)
    return np.concatenate([u, np.ones((n, n), np.float32)], axis=1)


def _sb_kernel(q_ref, k_ref, v_ref, z_ref, u_ref, o_ref, q_scr, c_scr, o_scr, cmin_ref):
    n = SB_BLOCK
    pairs = q_ref.shape[1] // n
    prs = range(pairs)
    i = pl.program_id(1)
    lane = lax.broadcasted_iota(jnp.int32, (2 * n, n), 1)
    row = lax.broadcasted_iota(jnp.int32, (2 * n, n), 0)
    strict = lane < (row & (n - 1))
    head_a = lax.broadcasted_iota(jnp.int32, (n, n), 1) < SB_HEAD_DIM

    for p in prs:
        q = q_ref[:, p * n:(p + 1) * n].astype(F32) * (SB_HEAD_DIM ** -0.5)
        q_scr[p, 0:n, :] = jnp.where(head_a, q, 0.0).astype(BF16)
        q_scr[p, n:2 * n, :] = jnp.where(head_a, 0.0, q).astype(BF16)

    def visit(blocks, fresh):
        units = [(p, t) for p in prs for t in range(len(blocks))]
        kv = {}
        for t, (j, _) in enumerate(blocks):
            rows = pl.ds(pl.multiple_of(j * n, n), n)
            for p in prs:
                kv[p, t] = (k_ref[rows, p * n:(p + 1) * n], v_ref[rows, p * n:(p + 1) * n])
        logits = {u: _dot_nt(q_scr[u[0]], kv[u][0]) for u in units}
        sums = {}
        for u in units:
            p, t = u
            x = logits[u]
            drop = jnp.maximum(x, 0.0) + jnp.log(1.0 + jnp.exp2(jnp.abs(x) * (-LOG2E)))
            if blocks[t][1]:
                drop = jnp.where(strict, drop, 0.0)
            sums[u] = _dot(drop, u_ref[...])
        cmin = None
        pvs = []
        for p in prs:
            c = None if fresh else c_scr[p]
            pv = None
            for t in range(len(blocks)):
                r = sums[p, t]
                total = r[:, :n] if c is None else r[:, :n] + c
                w = jnp.exp(logits[p, t] - total)
                if blocks[t][1]:
                    w = jnp.where(strict, w, 0.0)
                term = _dot(w, kv[p, t][1])
                pv = term if pv is None else pv + term
                c = r[:, n:] if c is None else c + r[:, n:]
            c_scr[p] = c
            pvs.append(pv)
            cmin = c if cmin is None else jnp.minimum(cmin, c)
        for p in prs:
            o_scr[p] = pvs[p] if fresh else o_scr[p] + pvs[p]
        return jnp.min(cmin)

    @pl.when(i == 0)
    def _():
        cmin_ref[0] = visit([(i, True)], True)

    @pl.when(i == 1)
    def _():
        cmin_ref[0] = visit([(i, True), (i - 1, False)], True)

    @pl.when(i >= 2)
    def _():
        cmin_ref[0] = visit([(i, True), (i - 1, False), (i - 2, False)], True)

    def cond(carry):
        j, cmin = carry
        return jnp.logical_and(j >= 0, cmin < -SB_SKIP_LOG)

    def body(carry):
        j, _ = carry
        return j - 1, visit([(j, False)], False)

    lax.while_loop(cond, body, (i - 3, cmin_ref[0]))
    for p in range(pairs):
        o = jnp.where(head_a, o_scr[p, 0:n, :], o_scr[p, n:2 * n, :])
        cols = slice(p * n, (p + 1) * n)
        o_ref[:, cols] = (o * _silu(z_ref[:, cols].astype(F32))).astype(o_ref.dtype)


def _sb_attn(proj3):
    bsz, seq, _ = proj3.shape
    n = SB_BLOCK
    w = BRANCH_WIDTH
    base = 7
    u2 = _sb_constants()
    return pl.pallas_call(
        _sb_kernel,
        out_shape=jax.ShapeDtypeStruct((bsz, seq, w), BF16),
        grid=(bsz, seq // n),
        in_specs=[
            pl.BlockSpec((None, n, w), lambda b, i: (b, i, base)),
            pl.BlockSpec((None, seq, w), lambda b, i: (b, 0, base + 1)),
            pl.BlockSpec((None, seq, w), lambda b, i: (b, 0, base + 2)),
            pl.BlockSpec((None, n, w), lambda b, i: (b, i, base + 3)),
            pl.BlockSpec(u2.shape, lambda b, i: (0, 0)),
        ],
        out_specs=pl.BlockSpec((None, n, w), lambda b, i: (b, i, 0)),
        scratch_shapes=[pltpu.VMEM((w // n, 2 * n, n), BF16),
                        pltpu.VMEM((w // n, 2 * n, n), F32),
                        pltpu.VMEM((w // n, 2 * n, n), F32),
                        pltpu.SMEM((1,), F32)],
        compiler_params=pltpu.CompilerParams(
            dimension_semantics=("arbitrary", "arbitrary"),
            vmem_limit_bytes=_mib(40)),
        name="stick_breaking",
    )(proj3, proj3, proj3, proj3, jnp.asarray(u2, BF16))


def _merge_kernel(ya_ref, yb_ref, yc_ref, g0, g1, g2, g3, g4, g5, x_ref, gate_ref, wb_ref,
                  wo_ref, fnw_ref, o_ref, *, final):
    w = BRANCH_WIDTH
    ys = (ya_ref[...], yb_ref[...], yc_ref[...])
    gl = ((g0, g1), (g2, g3), (g4, g5))
    halves = []
    for half in range(2):
        acc = None
        for nb in range(N_BRANCH):
            br = jnp.dot(ys[nb], wb_ref[nb, :, half * w:(half + 1) * w],
                         preferred_element_type=F32)
            term = jax.nn.sigmoid(gl[nb][half][...].astype(F32)) * br
            acc = term if acc is None else acc + term
        halves.append(acc.astype(BF16))
    merged = jnp.concatenate(halves, axis=1)
    out = jnp.dot(merged, wo_ref[...], preferred_element_type=F32)
    xn = x_ref[...] + gate_ref[...] * out
    if final:
        ms = jnp.mean(xn * xn, axis=-1, keepdims=True)
        xn = xn * lax.rsqrt(ms + EPS) * fnw_ref[...]
    o_ref[...] = xn


def _merge(ya, yb, yc, proj, x2, mod5, wb_bf16, wo_bf16, final_norm_w, layer, seq, final):
    m, d = x2.shape
    w = BRANCH_WIDTH
    tm = 512
    per_seq = seq // tm
    gate0 = 11
    ycol = lambda: pl.BlockSpec((tm, w), lambda i: (i, 0))
    gcol = lambda cb: pl.BlockSpec((tm, w), lambda i: (i, gate0 + cb))
    return pl.pallas_call(
        functools.partial(_merge_kernel, final=final),
        out_shape=jax.ShapeDtypeStruct((m, d), F32),
        grid=(m // tm,),
        in_specs=[
            ycol(), ycol(), ycol(),
            gcol(0), gcol(1), gcol(2), gcol(3), gcol(4), gcol(5),
            pl.BlockSpec((tm, d), lambda i: (i, 0)),
            pl.BlockSpec((None, None, None, 1, d), lambda i: (layer, i // per_seq, 2, 0, 0)),
            pl.BlockSpec((None, N_BRANCH, w, d), lambda i: (layer, 0, 0, 0)),
            pl.BlockSpec((None, d, d), lambda i: (layer, 0, 0)),
            pl.BlockSpec((1, d), lambda i: (0, 0)),
        ],
        out_specs=pl.BlockSpec((tm, d), lambda i: (i, 0)),
        compiler_params=pltpu.CompilerParams(
            dimension_semantics=("arbitrary",),
            vmem_limit_bytes=_mib(48)),
        name="merge_out",
    )(ya, yb, yc, proj, proj, proj, proj, proj, proj, x2, mod5, wb_bf16, wo_bf16,
      final_norm_w.reshape(1, d))


def kernel(x, c, ada_w, ada_b, norm_w, w_in, hgrn_lb, hgrn_norm_w, conv_w, conv_b, conv_ln_w,
           conv_ln_b, w_branch, w_out, final_norm_w):
    bsz, seq, d = x.shape
    depth = ada_w.shape[0]
    m = bsz * seq
    mod = _ada_mod(c, ada_w, ada_b)
    mod5 = mod.reshape(depth, bsz, 3, 1, d)
    w_in_b = w_in.astype(BF16)
    wb_b = w_branch.astype(BF16)
    wo_b = w_out.astype(BF16)
    x2 = x.reshape(m, d)
    for layer in range(depth):
        proj = _inproj(x2, mod5, norm_w, w_in_b, layer, seq)
        proj3 = proj.reshape(bsz, seq, proj.shape[-1])
        ya = _hgrn(proj3, hgrn_lb, hgrn_norm_w, layer)
        yb =_conv(proj3, conv_w, conv_b, conv_ln_w, conv_ln_b, layer)
        yc = _sb_attn(proj3)
        x2 = _merge(ya.reshape(m, -1), yb.reshape(m, -1), yc.reshape(m, -1), proj, x2, mod5,
                    wb_b, wo_b, final_norm_w, layer, seq, final=(layer == depth - 1))
    return x2.reshape(bsz, seq, d)
```

```python
import functools

import numpy as np
import jax
import jax.numpy as jnp
from jax import lax
from jax.experimental import pallas as pl
from jax.experimental.pallas import tpu as pltpu

F32 = jnp.float32
BF16 = jnp.bfloat16

LANES_V7X = 128
SUBLANES_V7X = 8
VMEM_BYTES_V7X = 64 * 1024 * 1024

EPS = 1e-6
TINY = 1e-30
LOG2E = 1.4426950408889634
BRANCH_WIDTH = 512
HGRN_HEADS = 4
HGRN_HEAD_DIM = BRANCH_WIDTH // HGRN_HEADS
HGRN_CHUNK = 64
HGRN_LEVELS = 6
_HGRN_MXU_LEVELS = 3
HGRN_DIRECT_MAX_LOG2 = 120.0
CONV_WIDTH = 31
CONV_HALO = 32
SB_HEADS = 8
SB_HEAD_DIM = BRANCH_WIDTH // SB_HEADS
SB_BLOCK = 128
SB_SKIP_LOG = -104.0
N_BRANCH = 3


def _mib(n):
    return int(n) * 1024 * 1024


def _silu(x):
    return x * jax.nn.sigmoid(x)


def _dot(a, b):
    return jnp.dot(a.astype(BF16), b.astype(BF16), preferred_element_type=F32)


def _dot_nt(a, b):
    return lax.dot_general(a.astype(BF16), b.astype(BF16), (((1,), (1,)), ((), ())),
                           preferred_element_type=F32)


def _dot_tn(a, b):
    return lax.dot_general(a.astype(BF16), b.astype(BF16), (((0,), (0,)), ((), ())),
                           preferred_element_type=F32)


def _split3(x):
    hi = x.astype(BF16)
    r = x - hi.astype(F32)
    mid = r.astype(BF16)
    lo = (r - mid.astype(F32)).astype(BF16)
    return hi, mid, lo


def _ada_kernel(c_ref, w_ref, b_ref, o_ref):
    c = c_ref[...]
    ca = _silu(c)
    w = w_ref[...]
    ch = ca.astype(BF16)
    cl = (ca - ch.astype(F32)).astype(BF16)
    wh = w.astype(BF16)
    wl = (w - wh.astype(F32)).astype(BF16)
    acc = (jnp.dot(ch, wh, preferred_element_type=F32)
           + jnp.dot(ch, wl, preferred_element_type=F32)
           + jnp.dot(cl, wh, preferred_element_type=F32))
    o_ref[...] = acc + b_ref[...]


def _ada_mod(c, ada_w, ada_b):
    depth, d, n = ada_w.shape
    bsz = c.shape[0]
    tn = n // 4
    return pl.pallas_call(
        _ada_kernel,
        out_shape=jax.ShapeDtypeStruct((depth, bsz, n), F32),
        grid=(depth, n // tn),
        in_specs=[
            pl.BlockSpec((bsz, d), lambda l, j: (0, 0)),
            pl.BlockSpec((None, d, tn), lambda l, j: (l, 0, j)),
            pl.BlockSpec((None, 1, tn), lambda l, j: (l, 0, j)),
        ],
        out_specs=pl.BlockSpec((None, bsz, tn), lambda l, j: (l, 0, j)),
        compiler_params=pltpu.CompilerParams(
            dimension_semantics=("arbitrary", "arbitrary"),
            vmem_limit_bytes=_mib(32)),
        name="ada_mod",
    )(c, ada_w, ada_b.reshape(depth, 1, n))


_INPROJ_NORM_ROWS = 256
_INPROJ_MM_ROWS = 1024


def _inproj_kernel(x_ref, shift_ref, scale_ref, nw_ref, w_ref, o_ref, h_scr):
    tm = x_ref.shape[0]

    @pl.when(pl.program_id(1) == 0)
    def _():
        gain = nw_ref[...] * (1.0 + scale_ref[...])
        shift = shift_ref[...]

        def body(i, carry):
            r0 = pl.multiple_of(i * _INPROJ_NORM_ROWS, _INPROJ_NORM_ROWS)
            x = x_ref[pl.ds(r0, _INPROJ_NORM_ROWS), :]
            ms = jnp.mean(x * x, axis=-1, keepdims=True)
            h = (x * lax.rsqrt(ms + EPS)) * gain + shift
            h_scr[pl.ds(r0, _INPROJ_NORM_ROWS), :] = h.astype(BF16)
            return carry

        lax.fori_loop(0, tm // _INPROJ_NORM_ROWS, body, 0)

    def mm(i, carry):
        r0 = pl.multiple_of(i * _INPROJ_MM_ROWS, _INPROJ_MM_ROWS)
        o_ref[pl.ds(r0, _INPROJ_MM_ROWS), :] = jnp.dot(
            h_scr[pl.ds(r0, _INPROJ_MM_ROWS), :], w_ref[...],
            preferred_element_type=F32).astype(o_ref.dtype)
        return carry

    lax.fori_loop(0, tm // _INPROJ_MM_ROWS, mm, 0)


def _inproj(x2, mod5, norm_w, w_in_bf16, layer, seq):
    m, d = x2.shape
    n = w_in_bf16.shape[-1]
    tm = seq
    tn = n // 4
    per_seq = seq // tm
    return pl.pallas_call(
        _inproj_kernel,
        out_shape=jax.ShapeDtypeStruct((m, n), BF16),
        grid=(m // tm, n // tn),
        in_specs=[
            pl.BlockSpec((tm, d), lambda i, j: (i, 0)),
            pl.BlockSpec((None, None, None, 1, d), lambda i, j: (layer, i // per_seq, 0, 0, 0)),
            pl.BlockSpec((None, None, None, 1, d), lambda i, j: (layer, i // per_seq, 1, 0, 0)),
            pl.BlockSpec((None, 1, d), lambda i, j: (layer, 0, 0)),
            pl.BlockSpec((None, d, tn), lambda i, j: (layer, 0, j)),
        ],
        out_specs=pl.BlockSpec((tm, tn), lambda i, j: (i, j)),
        scratch_shapes=[pltpu.VMEM((tm, d), BF16)],
        compiler_params=pltpu.CompilerParams(
            dimension_semantics=("arbitrary", "arbitrary"),
            vmem_limit_bytes=_mib(56)),
        name="inproj",
    )(x2, mod5, mod5, norm_w.reshape(norm_w.shape[0], 1, d), w_in_bf16)


def _hgrn_constants():
    L = HGRN_CHUNK
    tri = np.tril(np.ones((L, L), np.float32))
    blocks = [tri]
    r = np.arange(L)
    for l in range(_HGRN_MXU_LEVELS):
        h = 1 << l
        p = (r // (2 * h)) * (2 * h) + h - 1
        sign = np.where(r > p, 1.0, -1.0).astype(np.float32)[:, None]
        blocks.append(sign * (tri - tri[p]))
    c_all = np.concatenate(blocks, axis=0)
    c_all = np.concatenate([c_all, c_all, c_all], axis=1)
    t = r[:, None]
    s = r[None, :]
    x = t ^ s
    lev = np.where(x > 0, np.floor(np.log2(np.maximum(x, 1))).astype(np.int32), HGRN_LEVELS)
    lev = np.where(s > t, -1, lev).astype(np.int32)
    return c_all, lev


def _hgrn_kernel(q_ref, f_ref, i_ref, z_ref, lb_ref, nw_ref, c_ref, lev_ref, ind_ref, o_ref,
                 st_ref, g_scr, *, layer):
    L = HGRN_CHUNK
    K = HGRN_HEAD_DIM
    n_chunks = q_ref.shape[0] // L

    @pl.when(pl.program_id(1) == 0)
    def _():
        st_ref[...] = jnp.zeros_like(st_ref)

    lb_all = lb_ref[...]
    lb_exp = jnp.exp(lb_all - jnp.max(lb_all, axis=0, keepdims=True))
    lb_soft = lb_exp / jnp.sum(lb_exp, axis=0, keepdims=True)
    lower = jnp.zeros((1, lb_all.shape[1]), F32)
    for l in range(1, layer + 1):
        lower = lower + lb_soft[l:l + 1, :]
    one_m_lower = 1.0 - lower
    nw = nw_ref[...]

    f_all = lower + one_m_lower * jax.nn.sigmoid(f_ref[...].astype(F32))
    g_all = jnp.log(jnp.maximum(f_all, TINY)) * LOG2E
    g_scr[...] = g_all
    totals = jnp.dot(ind_ref[...], g_all.astype(BF16), preferred_element_type=F32)
    n_ind = ind_ref.shape[0] // 2
    worst = -jnp.min(totals[0:n_ind])
    worst_half = -jnp.min(totals[n_ind:])

    def body(c, carry, path):
        direct = path != "general"
        r0 = pl.multiple_of(c * L, L)
        rows = pl.ds(r0, L)
        q = _silu(q_ref[rows, :].astype(F32)) * (K ** -0.5)
        g = g_scr[rows, :]
        k = one_m_lower * jax.nn.sigmoid(-f_ref[rows, :].astype(F32))
        v = i_ref[rows, :]
        z = z_ref[rows, :].astype(F32)
        cmat = c_ref[0:L, :] if direct else c_ref[...]
        dall = jnp.dot(cmat, jnp.concatenate(_split3(g), axis=0),
                       preferred_element_type=F32)
        lev = lev_ref[...]
        heads = range(HGRN_HEADS)
        hcols = [slice(h * K, (h + 1) * K) for h in heads]

        if direct:
            pair, o_inter, upd, decay = [], [], [], []
            for h in heads:
                cols = hcols[h]
                qh, kh, vh = q[:, cols], k[:, cols], v[:, cols]
                b = dall[:, cols]
                b_last = b[L - 1:L, :]
                if path == "direct":
                    qe = qh * jnp.exp2(b)
                    pair.append(_dot_nt(qe, kh * jnp.exp2(-b)))
                else:
                    half = L // 2
                    b_mid = b[half - 1:half, :]
                    d = jnp.concatenate([b[0:half, :], b[half:L, :] - b_mid], axis=0)
                    qf = qh * jnp.exp2(d)
                    qe = jnp.concatenate([qf[0:half, :], qf[half:L, :] * jnp.exp2(b_mid)], axis=0)
                    same = _dot_nt(qf, kh * jnp.exp2(-d))
                    cross = _dot_nt(qf, kh * jnp.exp2(b_mid - b))
                    pair.append(jnp.where(lev == HGRN_LEVELS - 1, cross, same))
                o_inter.append(_dot_nt(qe, st_ref[h]))
                upd.append(_dot_tn(vh, kh * jnp.exp2(b_last - b)))
                decay.append(jnp.exp2(b_last))
            outs = [o_inter[h] + _dot(jnp.where(lev >= 0, pair[h], 0.0), v[:, hcols[h]])
                    for h in heads]
            for h in heads:
                st_ref[h] = st_ref[h] * decay[h] + upd[h]
                o = outs[h]
                on = o * lax.rsqrt(jnp.mean(o * o, axis=-1, keepdims=True) + EPS) * nw
                o_ref[rows, hcols[h]] = (on * _silu(z[:, hcols[h]])).astype(o_ref.dtype)
            return carry

        def boundary_gap(b, l, cols):
            if l < _HGRN_MXU_LEVELS:
                return dall[(l + 1) * L:(l + 2) * L, cols]
            hw = 1 << l
            parts = []
            for s in range(0, L, 2 * hw):
                bp = b[s + hw - 1:s + hw, :]
                parts += [bp - b[s:s + hw, :], b[s + hw:s + 2 * hw, :] - bp]
            return jnp.concatenate(parts, axis=0)

        pair, o_inter, upd, decay = [], [], [], []
        for h in heads:
            cols = hcols[h]
            qh, kh, vh = q[:, cols], k[:, cols], v[:, cols]
            b = dall[0:L, cols]
            b_last = b[L - 1:L, :]
            o_inter.append(_dot_nt(qh * jnp.exp2(b), st_ref[h]))
            upd.append(_dot_tn(vh, kh * jnp.exp2(b_last - b)))
            decay.append(jnp.exp2(b_last))
            prods = [_dot_nt(qh, kh)]
            for l in range(HGRN_LEVELS):
                fac = jnp.exp2(boundary_gap(b, l, cols))
                prods.append(_dot_nt(qh * fac, kh * fac))
            pair.append(prods)
        outs = []
        for h in heads:
            a = jnp.where(lev == HGRN_LEVELS, pair[h][0], 0.0)
            for l in range(HGRN_LEVELS):
                a = jnp.where(lev == l, pair[h][l + 1], a)
            outs.append(o_inter[h] + _dot(a, v[:, hcols[h]]))
        for h in heads:
            st_ref[h] = st_ref[h] * decay[h] + upd[h]
            o = outs[h]
            on = o * lax.rsqrt(jnp.mean(o * o, axis=-1, keepdims=True) + EPS) * nw
            o_ref[rows, hcols[h]] = (on * _silu(z[:, hcols[h]])).astype(o_ref.dtype)
        return carry

    direct_ok = worst < HGRN_DIRECT_MAX_LOG2
    halves_ok = jnp.logical_and(jnp.logical_not(direct_ok), worst_half < HGRN_DIRECT_MAX_LOG2)

    @pl.when(direct_ok)
    def _():
        lax.fori_loop(0, n_chunks, functools.partial(body, path="direct"), 0, unroll=8)

    @pl.when(halves_ok)
    def _():
        lax.fori_loop(0, n_chunks, functools.partial(body, path="halves"), 0, unroll=8)

    @pl.when(jnp.logical_not(jnp.logical_or(direct_ok, halves_ok)))
    def _():
        lax.fori_loop(0, n_chunks, functools.partial(body, path="general"), 0, unroll=4)


def _hgrn(proj3, hgrn_lb, hgrn_norm_w, layer):
    bsz, seq, _ = proj3.shape
    w = BRANCH_WIDTH
    tb = 512
    c_all, lev = _hgrn_constants()
    half = HGRN_CHUNK // 2
    n_ind = tb // half
    ind = np.zeros((2 * n_ind, tb), np.float32)
    for c in range(tb // HGRN_CHUNK):
        ind[c, c * HGRN_CHUNK:(c + 1) * HGRN_CHUNK] = 1.0
    for c in range(n_ind):
        ind[n_ind + c, c * half:(c + 1) * half] = 1.0
    col = lambda cb: pl.BlockSpec((None, tb, w), lambda b, s: (b, s, cb))
    return pl.pallas_call(
        functools.partial(_hgrn_kernel, layer=layer),
        out_shape=jax.ShapeDtypeStruct((bsz, seq, w), BF16),
        grid=(bsz, seq // tb),
        in_specs=[
            col(0), col(1), col(2), col(3),
            pl.BlockSpec(hgrn_lb.shape, lambda b, s: (0, 0)),
            pl.BlockSpec((None, 1, HGRN_HEAD_DIM), lambda b, s: (layer, 0, 0)),
            pl.BlockSpec(c_all.shape, lambda b, s: (0, 0)),
            pl.BlockSpec(lev.shape, lambda b, s: (0, 0)),
            pl.BlockSpec(ind.shape, lambda b, s: (0, 0)),
        ],
        out_specs=pl.BlockSpec((None, tb, w), lambda b, s: (b, s, 0)),
        scratch_shapes=[pltpu.VMEM((HGRN_HEADS, HGRN_HEAD_DIM, HGRN_HEAD_DIM), F32),
                        pltpu.VMEM((tb, w), F32)],
        compiler_params=pltpu.CompilerParams(
            dimension_semantics=("arbitrary", "arbitrary"),
            vmem_limit_bytes=_mib(32)),
        name="hgrn2",
    )(proj3, proj3, proj3, proj3, hgrn_lb,
      hgrn_norm_w.reshape(hgrn_norm_w.shape[0], 1, HGRN_HEAD_DIM),
      jnp.asarray(c_all, BF16), jnp.asarray(lev), jnp.asarray(ind, BF16))


_CONV_ROWS = 32
_CONV_GLU_ROWS = 64


def _conv_kernel(a_ref, g_ref, z_ref, w_ref, b_ref, lnw_ref, lnb_ref, o_ref, u_scr, wb_scr):
    ts = a_ref.shape[0]
    sub = SUBLANES_V7X

    @pl.when(pl.program_id(1) == 0)
    def _():
        u_scr[:, 0:CONV_HALO, :] = jnp.zeros((sub, CONV_HALO, u_scr.shape[2]), F32)

    @pl.when(pl.program_id(1) > 0)
    def _():
        for res in range(sub):
            u_scr[res, 0:CONV_HALO - res, :] = u_scr[res, ts:ts + CONV_HALO - res, :]

    for c in range(ts // _CONV_GLU_ROWS):
        r0 = c * _CONV_GLU_ROWS
        u = (a_ref[r0:r0 + _CONV_GLU_ROWS, :].astype(F32)
             * jax.nn.sigmoid(g_ref[r0:r0 + _CONV_GLU_ROWS, :].astype(F32)))
        for res in range(sub):
            u_scr[res, CONV_HALO - res + r0:CONV_HALO - res + r0 + _CONV_GLU_ROWS, :] = u

    wd = w_ref.shape[1]
    for j in range(CONV_WIDTH):
        wb_scr[j] = jnp.broadcast_to(w_ref[j:j + 1, :], (sub, wd))
    wb_scr[CONV_WIDTH] = jnp.broadcast_to(b_ref[...], (sub, wd))
    lnw = lnw_ref[...]
    lnb = lnb_ref[...]
    first = CONV_HALO - (CONV_WIDTH - 1)
    tiles = _CONV_ROWS // sub

    def chunk(c, carry):
        r0 = pl.multiple_of(c * _CONV_ROWS, _CONV_ROWS)
        acc = jnp.broadcast_to(wb_scr[CONV_WIDTH][None], (tiles, sub, wd))
        for j in range(CONV_WIDTH):
            res = (first + j) % sub
            win = u_scr[res, pl.ds(r0 + (first + j - res), _CONV_ROWS), :]
            acc = acc + wb_scr[j][None] * win.reshape(tiles, sub, wd)
        acc = acc.reshape(_CONV_ROWS, wd)
        mu = jnp.mean(acc, axis=-1, keepdims=True)
        d = acc - mu
        var = jnp.mean(d * d, axis=-1, keepdims=True)
        y = _silu(d * lax.rsqrt(var + EPS) * lnw + lnb)
        z = z_ref[pl.ds(r0, _CONV_ROWS), :].astype(F32)
        o_ref[pl.ds(r0, _CONV_ROWS), :] = (y * _silu(z)).astype(o_ref.dtype)
        return carry

    lax.fori_loop(0, ts // _CONV_ROWS, chunk, 0, unroll=4)


def _conv(proj3, conv_w, conv_b, ln_w, ln_b, layer):
    bsz, seq, _ = proj3.shape
    w = BRANCH_WIDTH
    ts = 256
    col = lambda cb: pl.BlockSpec((None, ts, w), lambda b, s: (b, s, cb))
    vec = lambda: pl.BlockSpec((None, 1, w), lambda b, s: (layer, 0, 0))
    depth = conv_w.shape[0]
    return pl.pallas_call(
        _conv_kernel,
        out_shape=jax.ShapeDtypeStruct((bsz, seq, w), BF16),
        grid=(bsz, seq // ts),
        in_specs=[
            col(4), col(5), col(6),
            pl.BlockSpec((None, CONV_WIDTH, w), lambda b, s: (layer, 0, 0)),
            vec(), vec(), vec(),
        ],
        out_specs=pl.BlockSpec((None, ts, w), lambda b, s: (b, s, 0)),
        scratch_shapes=[pltpu.VMEM((SUBLANES_V7X, CONV_HALO + ts, w), F32),
                        pltpu.VMEM((CONV_WIDTH + 1, SUBLANES_V7X, w), F32)],
        compiler_params=pltpu.CompilerParams(
            dimension_semantics=("arbitrary", "arbitrary"),
            vmem_limit_bytes=_mib(32)),
        name="conv_module",
    )(proj3, proj3, proj3, conv_w, conv_b.reshape(depth, 1, w), ln_w.reshape(depth, 1, w),
      ln_b.reshape(depth, 1, w))


def _sb_constants():
    n = SB_BLOCK
    u = (np.arange(n)[:, None] >= np.arange(n)[None, :]).astype(np.float32)
    return np.concatenate([u, np.ones((n, n), np.float32)], axis=1)


def _sb_kernel(q_ref, k_ref, v_ref, z_ref, u_ref, o_ref, q_scr, c_scr, o_scr, cmin_ref):
    n = SB_BLOCK
    pairs = q_ref.shape[1] // n
    prs = range(pairs)
    i = pl.program_id(1)
    lane = lax.broadcasted_iota(jnp.int32, (2 * n, n), 1)
    row = lax.broadcasted_iota(jnp.int32, (2 * n, n), 0)
    strict = lane < (row & (n - 1))
    head_a = lax.broadcasted_iota(jnp.int32, (n, n), 1) < SB_HEAD_DIM

    for p in prs:
        q = q_ref[:, p * n:(p + 1) * n].astype(F32) * (SB_HEAD_DIM ** -0.5)
        q_scr[p, 0:n, :] = jnp.where(head_a, q, 0.0).astype(BF16)
        q_scr[p, n:2 * n, :] = jnp.where(head_a, 0.0, q).astype(BF16)

    def visit(blocks, fresh):
        units = [(p, t) for p in prs for t in range(len(blocks))]
        kv = {}
        for t, (j, _) in enumerate(blocks):
            rows = pl.ds(pl.multiple_of(j * n, n), n)
            for p in prs:
                kv[p, t] = (k_ref[rows, p * n:(p + 1) * n], v_ref[rows, p * n:(p + 1) * n])
        logits = {u: _dot_nt(q_scr[u[0]], kv[u][0]) for u in units}
        sums = {}
        for u in units:
            p, t = u
            x = logits[u]
            drop = jnp.maximum(x, 0.0) + jnp.log(1.0 + jnp.exp2(jnp.abs(x) * (-LOG2E)))
            if blocks[t][1]:
                drop = jnp.where(strict, drop, 0.0)
            sums[u] = _dot(drop, u_ref[...])
        cmin = None
        pvs = []
        for p in prs:
            c = None if fresh else c_scr[p]
            pv = None
            for t in range(len(blocks)):
                r = sums[p, t]
                total = r[:, :n] if c is None else r[:, :n] + c
                w = jnp.exp(logits[p, t] - total)
                if blocks[t][1]:
                    w = jnp.where(strict, w, 0.0)
                term = _dot(w, kv[p, t][1])
                pv = term if pv is None else pv + term
                c = r[:, n:] if c is None else c + r[:, n:]
            c_scr[p] = c
            pvs.append(pv)
            cmin = c if cmin is None else jnp.minimum(cmin, c)
        for p in prs:
            o_scr[p] = pvs[p] if fresh else o_scr[p] + pvs[p]
        return jnp.min(cmin)

    @pl.when(i == 0)
    def _():
        cmin_ref[0] = visit([(i, True)], True)

    @pl.when(i == 1)
    def _():
        cmin_ref[0] = visit([(i, True), (i - 1, False)], True)

    @pl.when(i >= 2)
    def _():
        cmin_ref[0] = visit([(i, True), (i - 1, False), (i - 2, False)], True)

    def cond(carry):
        j, cmin = carry
        return jnp.logical_and(j >= 0, cmin < -SB_SKIP_LOG)

    def body(carry):
        j, _ = carry
        return j - 1, visit([(j, False)], False)

    lax.while_loop(cond, body, (i - 3, cmin_ref[0]))
    for p in range(pairs):
        o = jnp.where(head_a, o_scr[p, 0:n, :], o_scr[p, n:2 * n, :])
        cols = slice(p * n, (p + 1) * n)
        o_ref[:, cols] = (o * _silu(z_ref[:, cols].astype(F32))).astype(o_ref.dtype)


def _sb_attn(proj3):
    bsz, seq, _ = proj3.shape
    n = SB_BLOCK
    w = BRANCH_WIDTH
    base = 7
    u2 = _sb_constants()
    return pl.pallas_call(
        _sb_kernel,
        out_shape=jax.ShapeDtypeStruct((bsz, seq, w), BF16),
        grid=(bsz, seq // n),
        in_specs=[
            pl.BlockSpec((None, n, w), lambda b, i: (b, i, base)),
            pl.BlockSpec((None, seq, w), lambda b, i: (b, 0, base + 1)),
            pl.BlockSpec((None, seq, w), lambda b, i: (b, 0, base + 2)),
            pl.BlockSpec((None, n, w), lambda b, i: (b, i, base + 3)),
            pl.BlockSpec(u2.shape, lambda b, i: (0, 0)),
        ],
        out_specs=pl.BlockSpec((None, n, w), lambda b, i: (b, i, 0)),
        scratch_shapes=[pltpu.VMEM((w // n, 2 * n, n), BF16),
                        pltpu.VMEM((w // n, 2 * n, n), F32),
                        pltpu.VMEM((w // n, 2 * n, n), F32),
                        pltpu.SMEM((1,), F32)],
        compiler_params=pltpu.CompilerParams(
            dimension_semantics=("arbitrary", "arbitrary"),
            vmem_limit_bytes=_mib(40)),
        name="stick_breaking",
    )(proj3, proj3, proj3, proj3, jnp.asarray(u2, BF16))


def _merge_kernel(ya_ref, yb_ref, yc_ref, g0, g1, g2, g3, g4, g5, x_ref, gate_ref, wb_ref,
                  wo_ref, fnw_ref, o_ref, *, final):
    w = BRANCH_WIDTH
    ys = (ya_ref[...], yb_ref[...], yc_ref[...])
    gl = ((g0, g1), (g2, g3), (g4, g5))
    halves = []
    for half in range(2):
        acc = None
        for nb in range(N_BRANCH):
            br = jnp.dot(ys[nb], wb_ref[nb, :, half * w:(half + 1) * w],
                         preferred_element_type=F32)
            term = jax.nn.sigmoid(gl[nb][half][...].astype(F32)) * br
            acc = term if acc is None else acc + term
        halves.append(acc.astype(BF16))
    merged = jnp.concatenate(halves, axis=1)
    out = jnp.dot(merged, wo_ref[...], preferred_element_type=F32)
    xn = x_ref[...] + gate_ref[...] * out
    if final:
        ms = jnp.mean(xn * xn, axis=-1, keepdims=True)
        xn = xn * lax.rsqrt(ms + EPS) * fnw_ref[...]
    o_ref[...] = xn


def _merge(ya, yb, yc, proj, x2, mod5, wb_bf16, wo_bf16, final_norm_w, layer, seq, final):
    m, d = x2.shape
    w = BRANCH_WIDTH
    tm = 512
    per_seq = seq // tm
    gate0 = 11
    ycol = lambda: pl.BlockSpec((tm, w), lambda i: (i, 0))
    gcol = lambda cb: pl.BlockSpec((tm, w), lambda i: (i, gate0 + cb))
    return pl.pallas_call(
        functools.partial(_merge_kernel, final=final),
        out_shape=jax.ShapeDtypeStruct((m, d), F32),
        grid=(m // tm,),
        in_specs=[
            ycol(), ycol(), ycol(),
            gcol(0), gcol(1), gcol(2), gcol(3), gcol(4), gcol(5),
            pl.BlockSpec((tm, d), lambda i: (i, 0)),
            pl.BlockSpec((None, None, None, 1, d), lambda i: (layer, i // per_seq, 2, 0, 0)),
            pl.BlockSpec((None, N_BRANCH, w, d), lambda i: (layer, 0, 0, 0)),
            pl.BlockSpec((None, d, d), lambda i: (layer, 0, 0)),
            pl.BlockSpec((1, d), lambda i: (0, 0)),
        ],
        out_specs=pl.BlockSpec((tm, d), lambda i: (i, 0)),
        compiler_params=pltpu.CompilerParams(
            dimension_semantics=("arbitrary",),
            vmem_limit_bytes=_mib(48)),
        name="merge_out",
    )(ya, yb, yc, proj, proj, proj, proj, proj, proj, x2, mod5, wb_bf16, wo_bf16,
      final_norm_w.reshape(1, d))


def kernel(x, c, ada_w, ada_b, norm_w, w_in, hgrn_lb, hgrn_norm_w, conv_w, conv_b, conv_ln_w,
           conv_ln_b, w_branch, w_out, final_norm_w):
    bsz, seq, d = x.shape
    depth = ada_w.shape[0]
    m = bsz * seq
    mod = _ada_mod(c, ada_w, ada_b)
    mod5 = mod.reshape(depth, bsz, 3, 1, d)
    w_in_b = w_in.astype(BF16)
    wb_b = w_branch.astype(BF16)
    wo_b = w_out.astype(BF16)
    x2 = x.reshape(m, d)
    for layer in range(depth):
        proj = _inproj(x2, mod5, norm_w, w_in_b, layer, seq)
        proj3 = proj.reshape(bsz, seq, proj.shape[-1])
        ya = _hgrn(proj3, hgrn_lb, hgrn_norm_w, layer)
        yb =_conv(proj3, conv_w, conv_b, conv_ln_w, conv_ln_b, layer)
        yc = _sb_attn(proj3)
        x2 = _merge(ya.reshape(m, -1), yb.reshape(m, -1), yc.reshape(m, -1), proj, x2, mod5,
                    wb_b, wo_b, final_norm_w, layer, seq, final=(layer == depth - 1))
    return x2.reshape(bsz, seq, d)
```

```python
import functools

import numpy as np
import jax
import jax.numpy as jnp
from jax import lax
from jax.experimental import pallas as pl
from jax.experimental.pallas import tpu as pltpu

F32 = jnp.float32
BF16 = jnp.bfloat16

SUBLANES_V7X = 8
VMEM_BYTES_V7X = 64 * 1024 * 1024

EPS = 1e-6
TINY = 1e-30
LOG2E = 1.4426950408889634
BRANCH_WIDTH = 512
HGRN_HEADS = 4
HGRN_HEAD_DIM = BRANCH_WIDTH // HGRN_HEADS
HGRN_CHUNK = 64
HGRN_LEVELS = 6
_HGRN_MXU_LEVELS = 3
HGRN_DIRECT_MAX_LOG2 = 120.0
CONV_WIDTH = 31
CONV_HALO = 32
SB_HEADS = 8
SB_HEAD_DIM = BRANCH_WIDTH // SB_HEADS
SB_BLOCK = 128
SB_SKIP_LOG = -104.0
N_BRANCH = 3


_COL_HGRN, _COL_CONV, _COL_SB, _COL_GATE = 0, 4, 7, 11


def _mib(n):
    assert int(n) * 1024 * 1024 <= VMEM_BYTES_V7X
    return int(n) * 1024 * 1024


def _silu(x):
    return x * jax.nn.sigmoid(x)


def _dot(a, b):
    return jnp.dot(a.astype(BF16), b.astype(BF16), preferred_element_type=F32)


def _dot_nt(a, b):
    return lax.dot_general(a.astype(BF16), b.astype(BF16), (((1,), (1,)), ((), ())),
                           preferred_element_type=F32)


def _dot_tn(a, b):
    return lax.dot_general(a.astype(BF16), b.astype(BF16), (((0,), (0,)), ((), ())),
                           preferred_element_type=F32)


def _split3(x):
    hi = x.astype(BF16)
    r = x - hi.astype(F32)
    mid = r.astype(BF16)
    lo = (r - mid.astype(F32)).astype(BF16)
    return hi, mid, lo


def _ada_kernel(c_ref, w_ref, b_ref, o_ref):
    c = c_ref[...]
    ca = _silu(c)
    w = w_ref[...]
    ch = ca.astype(BF16)
    cl = (ca - ch.astype(F32)).astype(BF16)
    wh = w.astype(BF16)
    wl = (w - wh.astype(F32)).astype(BF16)
    acc = (jnp.dot(ch, wh, preferred_element_type=F32)
           + jnp.dot(ch, wl, preferred_element_type=F32)
           + jnp.dot(cl, wh, preferred_element_type=F32))
    o_ref[...] = acc + b_ref[...]


def _ada_mod(c, ada_w, ada_b):
    depth, d, n = ada_w.shape
    bsz = c.shape[0]
    tn = n // 4
    return pl.pallas_call(
        _ada_kernel,
        out_shape=jax.ShapeDtypeStruct((depth, bsz, n), F32),
        grid=(depth, n // tn),
        in_specs=[
            pl.BlockSpec((bsz, d), lambda l, j: (0, 0)),
            pl.BlockSpec((None, d, tn), lambda l, j: (l, 0, j)),
            pl.BlockSpec((None, 1, tn), lambda l, j: (l, 0, j)),
        ],
        out_specs=pl.BlockSpec((None, bsz, tn), lambda l, j: (l, 0, j)),
        compiler_params=pltpu.CompilerParams(
            dimension_semantics=("arbitrary", "arbitrary"),
            vmem_limit_bytes=_mib(32)),
        name="ada_mod",
    )(c, ada_w, ada_b.reshape(depth, 1, n))


_INPROJ_NORM_ROWS = 256
_INPROJ_MM_ROWS = 1024


def _inproj_kernel(x_ref, shift_ref, scale_ref, nw_ref, w_ref, o_ref, h_scr):
    tm = x_ref.shape[0]

    @pl.when(pl.program_id(1) == 0)
    def _():
        gain = nw_ref[...] * (1.0 + scale_ref[...])
        shift = shift_ref[...]

        def body(i, carry):
            r0 = pl.multiple_of(i * _INPROJ_NORM_ROWS, _INPROJ_NORM_ROWS)
            x = x_ref[pl.ds(r0, _INPROJ_NORM_ROWS), :]
            ms = jnp.mean(x * x, axis=-1, keepdims=True)
            h = (x * lax.rsqrt(ms + EPS)) * gain + shift
            h_scr[pl.ds(r0, _INPROJ_NORM_ROWS), :] = h.astype(BF16)
            return carry

        lax.fori_loop(0, tm // _INPROJ_NORM_ROWS, body, 0)

    def mm(i, carry):
        r0 = pl.multiple_of(i * _INPROJ_MM_ROWS, _INPROJ_MM_ROWS)
        o_ref[pl.ds(r0, _INPROJ_MM_ROWS), :] = jnp.dot(
            h_scr[pl.ds(r0, _INPROJ_MM_ROWS), :], w_ref[...],
            preferred_element_type=F32).astype(o_ref.dtype)
        return carry

    lax.fori_loop(0, tm // _INPROJ_MM_ROWS, mm, 0)


def _inproj(x2, mod5, norm_w, w_in_bf16, layer, seq):
    m, d = x2.shape
    n = w_in_bf16.shape[-1]
    tm = seq
    tn = n // 4
    per_seq = seq // tm
    return pl.pallas_call(
        _inproj_kernel,
        out_shape=jax.ShapeDtypeStruct((m, n), BF16),
        grid=(m // tm, n // tn),
        in_specs=[
            pl.BlockSpec((tm, d), lambda i, j: (i, 0)),
            pl.BlockSpec((None, None, None, 1, d), lambda i, j: (layer, i // per_seq, 0, 0, 0)),
            pl.BlockSpec((None, None, None, 1, d), lambda i, j: (layer, i // per_seq, 1, 0, 0)),
            pl.BlockSpec((None, 1, d), lambda i, j: (layer, 0, 0)),
            pl.BlockSpec((None, d, tn), lambda i, j: (layer, 0, j)),
        ],
        out_specs=pl.BlockSpec((tm, tn), lambda i, j: (i, j)),
        scratch_shapes=[pltpu.VMEM((tm, d), BF16)],
        compiler_params=pltpu.CompilerParams(
            dimension_semantics=("arbitrary", "arbitrary"),
            vmem_limit_bytes=_mib(56)),
        name="inproj",
    )(x2, mod5, mod5, norm_w.reshape(norm_w.shape[0], 1, d), w_in_bf16)


def _hgrn_constants():
    L = HGRN_CHUNK
    tri = np.tril(np.ones((L, L), np.float32))
    blocks = [tri]
    r = np.arange(L)
    for l in range(_HGRN_MXU_LEVELS):
        h = 1 << l
        p = (r // (2 * h)) * (2 * h) + h - 1
        sign = np.where(r > p, 1.0, -1.0).astype(np.float32)[:, None]
        blocks.append(sign * (tri - tri[p]))
    c_all = np.concatenate(blocks, axis=0)
    c_all = np.concatenate([c_all, c_all, c_all], axis=1)
    t = r[:, None]
    s = r[None, :]
    x = t ^ s
    lev = np.where(x > 0, np.floor(np.log2(np.maximum(x, 1))).astype(np.int32), HGRN_LEVELS)
    lev = np.where(s > t, -1, lev).astype(np.int32)
    return c_all, lev


def _hgrn_kernel(q_ref, f_ref, i_ref, z_ref, lb_ref, nw_ref, c_ref, lev_ref, ind_ref, o_ref,
                 st_ref, g_scr, *, layer):
    L = HGRN_CHUNK
    K = HGRN_HEAD_DIM
    n_chunks = q_ref.shape[0] // L

    @pl.when(pl.program_id(1) == 0)
    def _():
        st_ref[...] = jnp.zeros_like(st_ref)

    lb_all = lb_ref[...]
    lb_exp = jnp.exp(lb_all - jnp.max(lb_all, axis=0, keepdims=True))
    lb_soft = lb_exp / jnp.sum(lb_exp, axis=0, keepdims=True)
    lower = jnp.zeros((1, lb_all.shape[1]), F32)
    for l in range(1, layer + 1):
        lower = lower + lb_soft[l:l + 1, :]
    one_m_lower = 1.0 - lower
    nw = nw_ref[...]

    f_all = lower + one_m_lower * jax.nn.sigmoid(f_ref[...].astype(F32))
    g_all = jnp.log(jnp.maximum(f_all, TINY)) * LOG2E
    g_scr[...] = g_all
    totals = jnp.dot(ind_ref[...], g_all.astype(BF16), preferred_element_type=F32)
    n_ind = ind_ref.shape[0] // 2
    worst = -jnp.min(totals[0:n_ind])
    worst_half = -jnp.min(totals[n_ind:])

    def body(c, carry, path):
        direct = path != "general"
        r0 = pl.multiple_of(c * L, L)
        rows = pl.ds(r0, L)
        q = _silu(q_ref[rows, :].astype(F32)) * (K ** -0.5)
        g = g_scr[rows, :]
        k = one_m_lower * jax.nn.sigmoid(-f_ref[rows, :].astype(F32))
        v = i_ref[rows, :]
        z = z_ref[rows, :].astype(F32)
        cmat = c_ref[0:L, :] if direct else c_ref[...]
        dall = jnp.dot(cmat, jnp.concatenate(_split3(g), axis=0),
                       preferred_element_type=F32)
        lev = lev_ref[...]
        heads = range(HGRN_HEADS)
        hcols = [slice(h * K, (h + 1) * K) for h in heads]

        if direct:
            pair, o_inter, upd, decay = [], [], [], []
            for h in heads:
                cols = hcols[h]
                qh, kh, vh = q[:, cols], k[:, cols], v[:, cols]
                b = dall[:, cols]
                b_last = b[L - 1:L, :]
                if path == "direct":
                    qe = qh * jnp.exp2(b)
                    pair.append(_dot_nt(qe, kh * jnp.exp2(-b)))
                else:
                    half = L // 2
                    b_mid = b[half - 1:half, :]
                    d = jnp.concatenate([b[0:half, :], b[half:L, :] - b_mid], axis=0)
                    qf = qh * jnp.exp2(d)
                    qe = jnp.concatenate([qf[0:half, :], qf[half:L, :] * jnp.exp2(b_mid)], axis=0)
                    same = _dot_nt(qf, kh * jnp.exp2(-d))
                    cross = _dot_nt(qf, kh * jnp.exp2(b_mid - b))
                    pair.append(jnp.where(lev == HGRN_LEVELS - 1, cross, same))
                o_inter.append(_dot_nt(qe, st_ref[h]))
                upd.append(_dot_tn(vh, kh * jnp.exp2(b_last - b)))
                decay.append(jnp.exp2(b_last))
            outs = [o_inter[h] + _dot(jnp.where(lev >= 0, pair[h], 0.0), v[:, hcols[h]])
                    for h in heads]
            for h in heads:
                st_ref[h] = st_ref[h] * decay[h] + upd[h]
                o = outs[h]
                on = o * lax.rsqrt(jnp.mean(o * o, axis=-1, keepdims=True) + EPS) * nw
                o_ref[rows, hcols[h]] = (on * _silu(z[:, hcols[h]])).astype(o_ref.dtype)
            return carry

        def boundary_gap(b, l, cols):
            if l < _HGRN_MXU_LEVELS:
                return dall[(l + 1) * L:(l + 2) * L, cols]
            hw = 1 << l
            parts = []
            for s in range(0, L, 2 * hw):
                bp = b[s + hw - 1:s + hw, :]
                parts += [bp - b[s:s + hw, :], b[s + hw:s + 2 * hw, :] - bp]
            return jnp.concatenate(parts, axis=0)

        pair, o_inter, upd, decay = [], [], [], []
        for h in heads:
            cols = hcols[h]
            qh, kh, vh = q[:, cols], k[:, cols], v[:, cols]
            b = dall[0:L, cols]
            b_last = b[L - 1:L, :]
            o_inter.append(_dot_nt(qh * jnp.exp2(b), st_ref[h]))
            upd.append(_dot_tn(vh, kh * jnp.exp2(b_last - b)))
            decay.append(jnp.exp2(b_last))
            prods = [_dot_nt(qh, kh)]
            for l in range(HGRN_LEVELS):
                fac = jnp.exp2(boundary_gap(b, l, cols))
                prods.append(_dot_nt(qh * fac, kh * fac))
            pair.append(prods)
        outs = []
        for h in heads:
            a = jnp.where(lev == HGRN_LEVELS, pair[h][0], 0.0)
            for l in range(HGRN_LEVELS):
                a = jnp.where(lev == l, pair[h][l + 1], a)
            outs.append(o_inter[h] + _dot(a, v[:, hcols[h]]))
        for h in heads:
            st_ref[h] = st_ref[h] * decay[h] + upd[h]
            o = outs[h]
            on = o * lax.rsqrt(jnp.mean(o * o, axis=-1, keepdims=True) + EPS) * nw
            o_ref[rows, hcols[h]] = (on * _silu(z[:, hcols[h]])).astype(o_ref.dtype)
        return carry

    direct_ok = worst < HGRN_DIRECT_MAX_LOG2
    halves_ok = jnp.logical_and(jnp.logical_not(direct_ok), worst_half < HGRN_DIRECT_MAX_LOG2)

    @pl.when(direct_ok)
    def _():
        lax.fori_loop(0, n_chunks, functools.partial(body, path="direct"), 0, unroll=8)

    @pl.when(halves_ok)
    def _():
        lax.fori_loop(0, n_chunks, functools.partial(body, path="halves"), 0, unroll=8)

    @pl.when(jnp.logical_not(jnp.logical_or(direct_ok, halves_ok)))
    def _():
        lax.fori_loop(0, n_chunks, functools.partial(body, path="general"), 0, unroll=4)


def _hgrn(proj3, hgrn_lb, hgrn_norm_w, layer):
    bsz, seq, _ = proj3.shape
    w = BRANCH_WIDTH
    tb = 512
    c_all, lev = _hgrn_constants()
    half = HGRN_CHUNK // 2
    n_ind = tb // half
    ind = np.zeros((2 * n_ind, tb), np.float32)
    for c in range(tb // HGRN_CHUNK):
        ind[c, c * HGRN_CHUNK:(c + 1) * HGRN_CHUNK] = 1.0
    for c in range(n_ind):
        ind[n_ind + c, c * half:(c + 1) * half] = 1.0
    col = lambda cb: pl.BlockSpec((None, tb, w), lambda b, s: (b, s, cb))
    return pl.pallas_call(
        functools.partial(_hgrn_kernel, layer=layer),
        out_shape=jax.ShapeDtypeStruct((bsz, seq, w), BF16),
        grid=(bsz, seq // tb),
        in_specs=[
            col(_COL_HGRN), col(_COL_HGRN + 1), col(_COL_HGRN + 2), col(_COL_HGRN + 3),
            pl.BlockSpec(hgrn_lb.shape, lambda b, s: (0, 0)),
            pl.BlockSpec((None, 1, HGRN_HEAD_DIM), lambda b, s: (layer, 0, 0)),
            pl.BlockSpec(c_all.shape, lambda b, s: (0, 0)),
            pl.BlockSpec(lev.shape, lambda b, s: (0, 0)),
            pl.BlockSpec(ind.shape, lambda b, s: (0, 0)),
        ],
        out_specs=pl.BlockSpec((None, tb, w), lambda b, s: (b, s, 0)),
        scratch_shapes=[pltpu.VMEM((HGRN_HEADS, HGRN_HEAD_DIM, HGRN_HEAD_DIM), F32),
                        pltpu.VMEM((tb, w), F32)],
        compiler_params=pltpu.CompilerParams(
            dimension_semantics=("arbitrary", "arbitrary"),
            vmem_limit_bytes=_mib(32)),
        name="hgrn2",
    )(proj3, proj3, proj3, proj3, hgrn_lb,
      hgrn_norm_w.reshape(hgrn_norm_w.shape[0], 1, HGRN_HEAD_DIM),
      jnp.asarray(c_all, BF16), jnp.asarray(lev), jnp.asarray(ind, BF16))


_CONV_ROWS = 32
_CONV_GLU_ROWS = 64


def _conv_kernel(a_ref, g_ref, z_ref, w_ref, b_ref, lnw_ref, lnb_ref, o_ref, u_scr, wb_scr):
    ts = a_ref.shape[0]
    sub = SUBLANES_V7X

    @pl.when(pl.program_id(1) == 0)
    def _():
        u_scr[:, 0:CONV_HALO, :] = jnp.zeros((sub, CONV_HALO, u_scr.shape[2]), F32)

    @pl.when(pl.program_id(1) > 0)
    def _():
        for res in range(sub):
            u_scr[res, 0:CONV_HALO - res, :] = u_scr[res, ts:ts + CONV_HALO - res, :]

    for c in range(ts // _CONV_GLU_ROWS):
        r0 = c * _CONV_GLU_ROWS
        u = (a_ref[r0:r0 + _CONV_GLU_ROWS, :].astype(F32)
             * jax.nn.sigmoid(g_ref[r0:r0 + _CONV_GLU_ROWS, :].astype(F32)))
        for res in range(sub):
            u_scr[res, CONV_HALO - res + r0:CONV_HALO - res + r0 + _CONV_GLU_ROWS, :] = u

    wd = w_ref.shape[1]
    for j in range(CONV_WIDTH):
        wb_scr[j] = jnp.broadcast_to(w_ref[j:j + 1, :], (sub, wd))
    wb_scr[CONV_WIDTH] = jnp.broadcast_to(b_ref[...], (sub, wd))
    lnw = lnw_ref[...]
    lnb = lnb_ref[...]
    first = CONV_HALO - (CONV_WIDTH - 1)
    tiles = _CONV_ROWS // sub

    def chunk(c, carry):
        r0 = pl.multiple_of(c * _CONV_ROWS, _CONV_ROWS)
        acc = jnp.broadcast_to(wb_scr[CONV_WIDTH][None], (tiles, sub, wd))
        for j in range(CONV_WIDTH):
            res = (first + j) % sub
            win = u_scr[res, pl.ds(r0 + (first + j - res), _CONV_ROWS), :]
            acc = acc + wb_scr[j][None] * win.reshape(tiles, sub, wd)
        acc = acc.reshape(_CONV_ROWS, wd)
        mu = jnp.mean(acc, axis=-1, keepdims=True)
        d = acc - mu
        var = jnp.mean(d * d, axis=-1, keepdims=True)
        y = _silu(d * lax.rsqrt(var + EPS) * lnw + lnb)
        z = z_ref[pl.ds(r0, _CONV_ROWS), :].astype(F32)
        o_ref[pl.ds(r0, _CONV_ROWS), :] = (y * _silu(z)).astype(o_ref.dtype)
        return carry

    lax.fori_loop(0, ts // _CONV_ROWS, chunk, 0, unroll=4)


def _conv(proj3, conv_w, conv_b, ln_w, ln_b, layer):
    bsz, seq, _ = proj3.shape
    w = BRANCH_WIDTH
    ts = 256
    col = lambda cb: pl.BlockSpec((None, ts, w), lambda b, s: (b, s, cb))
    vec = lambda: pl.BlockSpec((None, 1, w), lambda b, s: (layer, 0, 0))
    depth = conv_w.shape[0]
    return pl.pallas_call(
        _conv_kernel,
        out_shape=jax.ShapeDtypeStruct((bsz, seq, w), BF16),
        grid=(bsz, seq // ts),
        in_specs=[
            col(_COL_CONV), col(_COL_CONV + 1), col(_COL_CONV + 2),
            pl.BlockSpec((None, CONV_WIDTH, w), lambda b, s: (layer, 0, 0)),
            vec(), vec(), vec(),
        ],
        out_specs=pl.BlockSpec((None, ts, w), lambda b, s: (b, s, 0)),
        scratch_shapes=[pltpu.VMEM((SUBLANES_V7X, CONV_HALO + ts, w), F32),
                        pltpu.VMEM((CONV_WIDTH + 1, SUBLANES_V7X, w), F32)],
        compiler_params=pltpu.CompilerParams(
            dimension_semantics=("arbitrary", "arbitrary"),
            vmem_limit_bytes=_mib(32)),
        name="conv_module",
    )(proj3, proj3, proj3, conv_w, conv_b.reshape(depth, 1, w), ln_w.reshape(depth, 1, w),
      ln_b.reshape(depth, 1, w))


def _sb_constants():
    n = SB_BLOCK
    u = (np.arange(n)[:, None] >= np.arange(n)[None, :]).astype(np.float32)
    return np.concatenate([u, np.ones((n, n), np.float32)], axis=1)


def _sb_kernel(q_ref, k_ref, v_ref, z_ref, u_ref, o_ref, q_scr, c_scr, o_scr, cmin_ref):
    n = SB_BLOCK
    pairs = q_ref.shape[1] // n
    prs = range(pairs)
    i = pl.program_id(1)
    lane = lax.broadcasted_iota(jnp.int32, (2 * n, n), 1)
    row = lax.broadcasted_iota(jnp.int32, (2 * n, n), 0)
    strict = lane < (row & (n - 1))
    head_a = lax.broadcasted_iota(jnp.int32, (n, n), 1) < SB_HEAD_DIM

    for p in prs:
        q = q_ref[:, p * n:(p + 1) * n].astype(F32) * (SB_HEAD_DIM ** -0.5)
        q_scr[p, 0:n, :] = jnp.where(head_a, q, 0.0).astype(BF16)
        q_scr[p, n:2 * n, :] = jnp.where(head_a, 0.0, q).astype(BF16)

    def visit(blocks, fresh):
        units = [(p, t) for p in prs for t in range(len(blocks))]
        kv = {}
        for t, (j, _) in enumerate(blocks):
            rows = pl.ds(pl.multiple_of(j * n, n), n)
            for p in prs:
                kv[p, t] = (k_ref[rows, p * n:(p + 1) * n], v_ref[rows, p * n:(p + 1) * n])
        logits = {u: _dot_nt(q_scr[u[0]], kv[u][0]) for u in units}
        sums = {}
        for u in units:
            p, t = u
            x = logits[u]
            drop = jnp.maximum(x, 0.0) + jnp.log(1.0 + jnp.exp2(jnp.abs(x) * (-LOG2E)))
            if blocks[t][1]:
                drop = jnp.where(strict, drop, 0.0)
            sums[u] = _dot(drop, u_ref[...])
        cmin = None
        pvs = []
        for p in prs:
            c = None if fresh else c_scr[p]
            pv = None
            for t in range(len(blocks)):
                r = sums[p, t]
                total = r[:, :n] if c is None else r[:, :n] + c
                w = jnp.exp(logits[p, t] - total)
                if blocks[t][1]:
                    w = jnp.where(strict, w, 0.0)
                term = _dot(w, kv[p, t][1])
                pv = term if pv is None else pv + term
                c = r[:, n:] if c is None else c + r[:, n:]
            c_scr[p] = c
            pvs.append(pv)
            cmin = c if cmin is None else jnp.minimum(cmin, c)
        for p in prs:
            o_scr[p] = pvs[p] if fresh else o_scr[p] + pvs[p]
        return jnp.min(cmin)

    @pl.when(i == 0)
    def _():
        cmin_ref[0] = visit([(i, True)], True)

    @pl.when(i == 1)
    def _():
        cmin_ref[0] = visit([(i, True), (i - 1, False)], True)

    @pl.when(i >= 2)
    def _():
        cmin_ref[0] = visit([(i, True), (i - 1, False), (i - 2, False)], True)

    def cond(carry):
        j, cmin = carry
        return jnp.logical_and(j >= 0, cmin < -SB_SKIP_LOG)

    def body(carry):
        j, _ = carry
        return j - 1, visit([(j, False)], False)

    lax.while_loop(cond, body, (i - 3, cmin_ref[0]))
    for p in range(pairs):
        o = jnp.where(head_a, o_scr[p, 0:n, :], o_scr[p, n:2 * n, :])
        cols = slice(p * n, (p + 1) * n)
        o_ref[:, cols] = (o * _silu(z_ref[:, cols].astype(F32))).astype(o_ref.dtype)


def _sb_attn(proj3):
    bsz, seq, _ = proj3.shape
    n = SB_BLOCK
    w = BRANCH_WIDTH
    base = _COL_SB
    u2 = _sb_constants()
    return pl.pallas_call(
        _sb_kernel,
        out_shape=jax.ShapeDtypeStruct((bsz, seq, w), BF16),
        grid=(bsz, seq // n),
        in_specs=[
            pl.BlockSpec((None, n, w), lambda b, i: (b, i, base)),
            pl.BlockSpec((None, seq, w), lambda b, i: (b, 0, base + 1)),
            pl.BlockSpec((None, seq, w), lambda b, i: (b, 0, base + 2)),
            pl.BlockSpec((None, n, w), lambda b, i: (b, i, base + 3)),
            pl.BlockSpec(u2.shape, lambda b, i: (0, 0)),
        ],
        out_specs=pl.BlockSpec((None, n, w), lambda b, i: (b, i, 0)),
        scratch_shapes=[pltpu.VMEM((w // n, 2 * n, n), BF16),
                        pltpu.VMEM((w // n, 2 * n, n), F32),
                        pltpu.VMEM((w // n, 2 * n, n), F32),
                        pltpu.SMEM((1,), F32)],
        compiler_params=pltpu.CompilerParams(
            dimension_semantics=("arbitrary", "arbitrary"),
            vmem_limit_bytes=_mib(40)),
        name="stick_breaking",
    )(proj3, proj3, proj3, proj3, jnp.asarray(u2, BF16))


def _merge_kernel(ya_ref, yb_ref, yc_ref, g0, g1, g2, g3, g4, g5, x_ref, gate_ref, wb_ref,
                  wo_ref, fnw_ref, o_ref, *, final):
    w = BRANCH_WIDTH
    ys = (ya_ref[...], yb_ref[...], yc_ref[...])
    gl = ((g0, g1), (g2, g3), (g4, g5))
    halves = []
    for half in range(2):
        acc = None
        for nb in range(N_BRANCH):
            br = jnp.dot(ys[nb], wb_ref[nb, :, half * w:(half + 1) * w],
                         preferred_element_type=F32)
            term = jax.nn.sigmoid(gl[nb][half][...].astype(F32)) * br
            acc = term if acc is None else acc + term
        halves.append(acc.astype(BF16))
    merged = jnp.concatenate(halves, axis=1)
    out = jnp.dot(merged, wo_ref[...], preferred_element_type=F32)
    xn = x_ref[...] + gate_ref[...] * out
    if final:
        ms = jnp.mean(xn * xn, axis=-1, keepdims=True)
        xn = xn * lax.rsqrt(ms + EPS) * fnw_ref[...]
    o_ref[...] = xn


def _merge(ya, yb, yc, proj, x2, mod5, wb_bf16, wo_bf16, final_norm_w, layer, seq, final):
    m, d = x2.shape
    w = BRANCH_WIDTH
    tm = 512
    per_seq = seq // tm
    gate0 = _COL_GATE
    ycol = lambda: pl.BlockSpec((tm, w), lambda i: (i, 0))
    gcol = lambda cb: pl.BlockSpec((tm, w), lambda i: (i, gate0 + cb))
    return pl.pallas_call(
        functools.partial(_merge_kernel, final=final),
        out_shape=jax.ShapeDtypeStruct((m, d), F32),
        grid=(m // tm,),
        in_specs=[
            ycol(), ycol(), ycol(),
            gcol(0), gcol(1), gcol(2), gcol(3), gcol(4), gcol(5),
            pl.BlockSpec((tm, d), lambda i: (i, 0)),
            pl.BlockSpec((None, None, None, 1, d), lambda i: (layer, i // per_seq, 2, 0, 0)),
            pl.BlockSpec((None, N_BRANCH, w, d), lambda i: (layer, 0, 0, 0)),
            pl.BlockSpec((None, d, d), lambda i: (layer, 0, 0)),
            pl.BlockSpec((1, d), lambda i: (0, 0)),
        ],
        out_specs=pl.BlockSpec((tm, d), lambda i: (i, 0)),
        compiler_params=pltpu.CompilerParams(
            dimension_semantics=("arbitrary",),
            vmem_limit_bytes=_mib(48)),
        name="merge_out",
    )(ya, yb, yc, proj, proj, proj, proj, proj, proj, x2, mod5, wb_bf16, wo_bf16,
      final_norm_w.reshape(1, d))


def kernel(x, c, ada_w, ada_b, norm_w, w_in, hgrn_lb, hgrn_norm_w, conv_w, conv_b, conv_ln_w,
           conv_ln_b, w_branch, w_out, final_norm_w):
    bsz, seq, d = x.shape
    depth = ada_w.shape[0]
    m = bsz * seq
    mod = _ada_mod(c, ada_w, ada_b)
    mod5 = mod.reshape(depth, bsz, 3, 1, d)
    w_in_b = w_in.astype(BF16)
    wb_b = w_branch.astype(BF16)
    wo_b = w_out.astype(BF16)
    x2 = x.reshape(m, d)
    for layer in range(depth):
        proj = _inproj(x2, mod5, norm_w, w_in_b, layer, seq)
        proj3 = proj.reshape(bsz, seq, proj.shape[-1])
        ya = _hgrn(proj3, hgrn_lb, hgrn_norm_w, layer)
        yb =_conv(proj3, conv_w, conv_b, conv_ln_w, conv_ln_b, layer)
        yc = _sb_attn(proj3)
        x2 = _merge(ya.reshape(m, -1), yb.reshape(m, -1), yc.reshape(m, -1), proj, x2, mod5,
                    wb_b, wo_b, final_norm_w, layer, seq, final=(layer == depth - 1))
    return x2.reshape(bsz, seq, d)
```

```python
import functools

import numpy as np
import jax
import jax.numpy as jnp
from jax import lax
from jax.experimental import pallas as pl
from jax.experimental.pallas import tpu as pltpu

F32 = jnp.float32
BF16 = jnp.bfloat16

SUBLANES_V7X = 8
VMEM_BYTES_V7X = 64 * 1024 * 1024

EPS = 1e-6
TINY = 1e-30
LOG2E = 1.4426950408889634
BRANCH_WIDTH = 512
HGRN_HEADS = 4
HGRN_HEAD_DIM = BRANCH_WIDTH // HGRN_HEADS
HGRN_CHUNK = 64
HGRN_LEVELS = 6
_HGRN_MXU_LEVELS = 3
HGRN_DIRECT_MAX_LOG2 = 120.0
CONV_WIDTH = 31
CONV_HALO = 32
SB_HEADS = 8
SB_HEAD_DIM = BRANCH_WIDTH // SB_HEADS
SB_BLOCK = 128
SB_SKIP_LOG = -104.0
N_BRANCH = 3


_COL_HGRN, _COL_CONV, _COL_SB, _COL_GATE = 0, 4, 7, 11


def _mib(n):
    assert int(n) * 1024 * 1024 <= VMEM_BYTES_V7X
    return int(n) * 1024 * 1024


def _silu(x):
    return x * jax.nn.sigmoid(x)


def _dot(a, b):
    return jnp.dot(a.astype(BF16), b.astype(BF16), preferred_element_type=F32)


def _dot_nt(a, b):
    return lax.dot_general(a.astype(BF16), b.astype(BF16), (((1,), (1,)), ((), ())),
                           preferred_element_type=F32)


def _dot_tn(a, b):
    return lax.dot_general(a.astype(BF16), b.astype(BF16), (((0,), (0,)), ((), ())),
                           preferred_element_type=F32)


def _split3(x):
    hi = x.astype(BF16)
    r = x - hi.astype(F32)
    mid = r.astype(BF16)
    lo = (r - mid.astype(F32)).astype(BF16)
    return hi, mid, lo


def _ada_kernel(c_ref, w_ref, b_ref, o_ref):
    c = c_ref[...]
    ca = _silu(c)
    w = w_ref[...]
    ch = ca.astype(BF16)
    cl = (ca - ch.astype(F32)).astype(BF16)
    wh = w.astype(BF16)
    wl = (w - wh.astype(F32)).astype(BF16)
    acc = (jnp.dot(ch, wh, preferred_element_type=F32)
           + jnp.dot(ch, wl, preferred_element_type=F32)
           + jnp.dot(cl, wh, preferred_element_type=F32))
    o_ref[...] = acc + b_ref[...]


def _ada_mod(c, ada_w, ada_b):
    depth, d, n = ada_w.shape
    bsz = c.shape[0]
    tn = n // 4
    return pl.pallas_call(
        _ada_kernel,
        out_shape=jax.ShapeDtypeStruct((depth, bsz, n), F32),
        grid=(depth, n // tn),
        in_specs=[
            pl.BlockSpec((bsz, d), lambda l, j: (0, 0)),
            pl.BlockSpec((None, d, tn), lambda l, j: (l, 0, j)),
            pl.BlockSpec((None, 1, tn), lambda l, j: (l, 0, j)),
        ],
        out_specs=pl.BlockSpec((None, bsz, tn), lambda l, j: (l, 0, j)),
        compiler_params=pltpu.CompilerParams(
            dimension_semantics=("arbitrary", "arbitrary"),
            vmem_limit_bytes=_mib(32)),
        name="ada_mod",
    )(c, ada_w, ada_b.reshape(depth, 1, n))


_INPROJ_NORM_ROWS = 256
_INPROJ_MM_ROWS = 1024


def _inproj_kernel(x_ref, shift_ref, scale_ref, nw_ref, w_ref, o_ref, h_scr):
    tm = x_ref.shape[0]

    @pl.when(pl.program_id(1) == 0)
    def _():
        gain = nw_ref[...] * (1.0 + scale_ref[...])
        shift = shift_ref[...]

        def body(i, carry):
            r0 = pl.multiple_of(i * _INPROJ_NORM_ROWS, _INPROJ_NORM_ROWS)
            x = x_ref[pl.ds(r0, _INPROJ_NORM_ROWS), :]
            ms = jnp.mean(x * x, axis=-1, keepdims=True)
            h = (x * lax.rsqrt(ms + EPS)) * gain + shift
            h_scr[pl.ds(r0, _INPROJ_NORM_ROWS), :] = h.astype(BF16)
            return carry

        lax.fori_loop(0, tm // _INPROJ_NORM_ROWS, body, 0)

    def mm(i, carry):
        r0 = pl.multiple_of(i * _INPROJ_MM_ROWS, _INPROJ_MM_ROWS)
        o_ref[pl.ds(r0, _INPROJ_MM_ROWS), :] = jnp.dot(
            h_scr[pl.ds(r0, _INPROJ_MM_ROWS), :], w_ref[...],
            preferred_element_type=F32).astype(o_ref.dtype)
        return carry

    lax.fori_loop(0, tm // _INPROJ_MM_ROWS, mm, 0)


def _inproj(x2, mod5, norm_w, w_in_bf16, layer, seq):
    m, d = x2.shape
    n = w_in_bf16.shape[-1]
    tm = seq
    tn = n // 4
    per_seq = seq // tm
    return pl.pallas_call(
        _inproj_kernel,
        out_shape=jax.ShapeDtypeStruct((m, n), BF16),
        grid=(m // tm, n // tn),
        in_specs=[
            pl.BlockSpec((tm, d), lambda i, j: (i, 0)),
            pl.BlockSpec((None, None, None, 1, d), lambda i, j: (layer, i // per_seq, 0, 0, 0)),
            pl.BlockSpec((None, None, None, 1, d), lambda i, j: (layer, i // per_seq, 1, 0, 0)),
            pl.BlockSpec((None, 1, d), lambda i, j: (layer, 0, 0)),
            pl.BlockSpec((None, d, tn), lambda i, j: (layer, 0, j)),
        ],
        out_specs=pl.BlockSpec((tm, tn), lambda i, j: (i, j)),
        scratch_shapes=[pltpu.VMEM((tm, d), BF16)],
        compiler_params=pltpu.CompilerParams(
            dimension_semantics=("arbitrary", "arbitrary"),
            vmem_limit_bytes=_mib(56)),
        name="inproj",
    )(x2, mod5, mod5, norm_w.reshape(norm_w.shape[0], 1, d), w_in_bf16)


def _hgrn_constants():
    L = HGRN_CHUNK
    tri = np.tril(np.ones((L, L), np.float32))
    blocks = [tri]
    r = np.arange(L)
    for l in range(_HGRN_MXU_LEVELS):
        h = 1 << l
        p = (r // (2 * h)) * (2 * h) + h - 1
        sign = np.where(r > p, 1.0, -1.0).astype(np.float32)[:, None]
        blocks.append(sign * (tri - tri[p]))
    c_all = np.concatenate(blocks, axis=0)
    c_all = np.concatenate([c_all, c_all, c_all], axis=1)
    t = r[:, None]
    s = r[None, :]
    x = t ^ s
    lev = np.where(x > 0, np.floor(np.log2(np.maximum(x, 1))).astype(np.int32), HGRN_LEVELS)
    lev = np.where(s > t, -1, lev).astype(np.int32)
    return c_all, lev


def _hgrn_kernel(q_ref, f_ref, i_ref, z_ref, lb_ref, nw_ref, c_ref, lev_ref, ind_ref, o_ref,
                 st_ref, g_scr, k_scr, *, layer):
    L = HGRN_CHUNK
    K = HGRN_HEAD_DIM
    n_chunks = q_ref.shape[0] // L

    @pl.when(pl.program_id(1) == 0)
    def _():
        st_ref[...] = jnp.zeros_like(st_ref)

    lb_all = lb_ref[...]
    lb_exp = jnp.exp(lb_all - jnp.max(lb_all, axis=0, keepdims=True))
    lb_soft = lb_exp / jnp.sum(lb_exp, axis=0, keepdims=True)
    lower = jnp.zeros((1, lb_all.shape[1]), F32)
    for l in range(1, layer + 1):
        lower = lower + lb_soft[l:l + 1, :]
    one_m_lower = 1.0 - lower
    nw = nw_ref[...]

    sig_all = jax.nn.sigmoid(f_ref[...].astype(F32))
    f_all = lower + one_m_lower * sig_all
    k_scr[...] = one_m_lower * (1.0 - sig_all)
    g_all = jnp.log(jnp.maximum(f_all, TINY)) * LOG2E
    g_scr[...] = g_all
    totals = jnp.dot(ind_ref[...], g_all.astype(BF16), preferred_element_type=F32)
    n_ind = ind_ref.shape[0] // 2
    worst = -jnp.min(totals[0:n_ind])
    worst_half = -jnp.min(totals[n_ind:])

    def body(c, carry, path):
        direct = path != "general"
        r0 = pl.multiple_of(c * L, L)
        rows = pl.ds(r0, L)
        q = _silu(q_ref[rows, :].astype(F32)) * (K ** -0.5)
        g = g_scr[rows, :]
        k = k_scr[rows, :]
        v = i_ref[rows, :]
        z = z_ref[rows, :].astype(F32)
        cmat = c_ref[0:L, :] if direct else c_ref[...]
        dall = jnp.dot(cmat, jnp.concatenate(_split3(g), axis=0),
                       preferred_element_type=F32)
        lev = lev_ref[...]
        heads = range(HGRN_HEADS)
        hcols = [slice(h * K, (h + 1) * K) for h in heads]

        if direct:
            pair, o_inter, upd, decay = [], [], [], []
            for h in heads:
                cols = hcols[h]
                qh, kh, vh = q[:, cols], k[:, cols], v[:, cols]
                b = dall[:, cols]
                b_last = b[L - 1:L, :]
                if path == "direct":
                    qe = qh * jnp.exp2(b)
                    pair.append(_dot_nt(qe, kh * jnp.exp2(-b)))
                else:
                    half = L // 2
                    b_mid = b[half - 1:half, :]
                    d = jnp.concatenate([b[0:half, :], b[half:L, :] - b_mid], axis=0)
                    qf = qh * jnp.exp2(d)
                    qe = jnp.concatenate([qf[0:half, :], qf[half:L, :] * jnp.exp2(b_mid)], axis=0)
                    same = _dot_nt(qf, kh * jnp.exp2(-d))
                    cross = _dot_nt(qf, kh * jnp.exp2(b_mid - b))
                    pair.append(jnp.where(lev == HGRN_LEVELS - 1, cross, same))
                o_inter.append(_dot_nt(qe, st_ref[h]))
                upd.append(_dot_tn(vh, kh * jnp.exp2(b_last - b)))
                decay.append(jnp.exp2(b_last))
            outs = [o_inter[h] + _dot(jnp.where(lev >= 0, pair[h], 0.0), v[:, hcols[h]])
                    for h in heads]
            for h in heads:
                st_ref[h] = st_ref[h] * decay[h] + upd[h]
                o = outs[h]
                on = o * lax.rsqrt(jnp.mean(o * o, axis=-1, keepdims=True) + EPS) * nw
                o_ref[rows, hcols[h]] = (on * _silu(z[:, hcols[h]])).astype(o_ref.dtype)
            return carry

        def boundary_gap(b, l, cols):
            if l < _HGRN_MXU_LEVELS:
                return dall[(l + 1) * L:(l + 2) * L, cols]
            hw = 1 << l
            parts = []
            for s in range(0, L, 2 * hw):
                bp = b[s + hw - 1:s + hw, :]
                parts += [bp - b[s:s + hw, :], b[s + hw:s + 2 * hw, :] - bp]
            return jnp.concatenate(parts, axis=0)

        pair, o_inter, upd, decay = [], [], [], []
        for h in heads:
            cols = hcols[h]
            qh, kh, vh = q[:, cols], k[:, cols], v[:, cols]
            b = dall[0:L, cols]
            b_last = b[L - 1:L, :]
            o_inter.append(_dot_nt(qh * jnp.exp2(b), st_ref[h]))
            upd.append(_dot_tn(vh, kh * jnp.exp2(b_last - b)))
            decay.append(jnp.exp2(b_last))
            prods = [_dot_nt(qh, kh)]
            for l in range(HGRN_LEVELS):
                fac = jnp.exp2(boundary_gap(b, l, cols))
                prods.append(_dot_nt(qh * fac, kh * fac))
            pair.append(prods)
        outs = []
        for h in heads:
            a = jnp.where(lev == HGRN_LEVELS, pair[h][0], 0.0)
            for l in range(HGRN_LEVELS):
                a = jnp.where(lev == l, pair[h][l + 1], a)
            outs.append(o_inter[h] + _dot(a, v[:, hcols[h]]))
        for h in heads:
            st_ref[h] = st_ref[h] * decay[h] + upd[h]
            o = outs[h]
            on = o * lax.rsqrt(jnp.mean(o * o, axis=-1, keepdims=True) + EPS) * nw
            o_ref[rows, hcols[h]] = (on * _silu(z[:, hcols[h]])).astype(o_ref.dtype)
        return carry

    direct_ok = worst < HGRN_DIRECT_MAX_LOG2
    halves_ok = jnp.logical_and(jnp.logical_not(direct_ok), worst_half < HGRN_DIRECT_MAX_LOG2)

    @pl.when(direct_ok)
    def _():
        lax.fori_loop(0, n_chunks, functools.partial(body, path="direct"), 0, unroll=8)

    @pl.when(halves_ok)
    def _():
        lax.fori_loop(0, n_chunks, functools.partial(body, path="halves"), 0, unroll=8)

    @pl.when(jnp.logical_not(jnp.logical_or(direct_ok, halves_ok)))
    def _():
        lax.fori_loop(0, n_chunks, functools.partial(body, path="general"), 0, unroll=4)


def _hgrn(proj3, hgrn_lb, hgrn_norm_w, layer):
    bsz, seq, _ = proj3.shape
    w = BRANCH_WIDTH
    tb = 512
    c_all, lev = _hgrn_constants()
    half = HGRN_CHUNK // 2
    n_ind = tb // half
    ind = np.zeros((2 * n_ind, tb), np.float32)
    for c in range(tb // HGRN_CHUNK):
        ind[c, c * HGRN_CHUNK:(c + 1) * HGRN_CHUNK] = 1.0
    for c in range(n_ind):
        ind[n_ind + c, c * half:(c + 1) * half] = 1.0
    col = lambda cb: pl.BlockSpec((None, tb, w), lambda b, s: (b, s, cb))
    return pl.pallas_call(
        functools.partial(_hgrn_kernel, layer=layer),
        out_shape=jax.ShapeDtypeStruct((bsz, seq, w), BF16),
        grid=(bsz, seq // tb),
        in_specs=[
            col(_COL_HGRN), col(_COL_HGRN + 1), col(_COL_HGRN + 2), col(_COL_HGRN + 3),
            pl.BlockSpec(hgrn_lb.shape, lambda b, s: (0, 0)),
            pl.BlockSpec((None, 1, HGRN_HEAD_DIM), lambda b, s: (layer, 0, 0)),
            pl.BlockSpec(c_all.shape, lambda b, s: (0, 0)),
            pl.BlockSpec(lev.shape, lambda b, s: (0, 0)),
            pl.BlockSpec(ind.shape, lambda b, s: (0, 0)),
        ],
        out_specs=pl.BlockSpec((None, tb, w), lambda b, s: (b, s, 0)),
        scratch_shapes=[pltpu.VMEM((HGRN_HEADS, HGRN_HEAD_DIM, HGRN_HEAD_DIM), F32),
                        pltpu.VMEM((tb, w), F32),
                        pltpu.VMEM((tb, w), F32)],
        compiler_params=pltpu.CompilerParams(
            dimension_semantics=("arbitrary", "arbitrary"),
            vmem_limit_bytes=_mib(32)),
        name="hgrn2",
    )(proj3, proj3, proj3, proj3, hgrn_lb,
      hgrn_norm_w.reshape(hgrn_norm_w.shape[0], 1, HGRN_HEAD_DIM),
      jnp.asarray(c_all, BF16), jnp.asarray(lev), jnp.asarray(ind, BF16))


_CONV_ROWS = 32
_CONV_GLU_ROWS = 64


def _conv_kernel(a_ref, g_ref, z_ref, w_ref, b_ref, lnw_ref, lnb_ref, o_ref, u_scr, wb_scr):
    ts = a_ref.shape[0]
    sub = SUBLANES_V7X

    @pl.when(pl.program_id(1) == 0)
    def _():
        u_scr[:, 0:CONV_HALO, :] = jnp.zeros((sub, CONV_HALO, u_scr.shape[2]), F32)

    @pl.when(pl.program_id(1) > 0)
    def _():
        for res in range(sub):
            u_scr[res, 0:CONV_HALO - res, :] = u_scr[res, ts:ts + CONV_HALO - res, :]

    for c in range(ts // _CONV_GLU_ROWS):
        r0 = c * _CONV_GLU_ROWS
        u = (a_ref[r0:r0 + _CONV_GLU_ROWS, :].astype(F32)
             * jax.nn.sigmoid(g_ref[r0:r0 + _CONV_GLU_ROWS, :].astype(F32)))
        for res in range(sub):
            u_scr[res, CONV_HALO - res + r0:CONV_HALO - res + r0 + _CONV_GLU_ROWS, :] = u

    wd = w_ref.shape[1]
    for j in range(CONV_WIDTH):
        wb_scr[j] = jnp.broadcast_to(w_ref[j:j + 1, :], (sub, wd))
    wb_scr[CONV_WIDTH] = jnp.broadcast_to(b_ref[...], (sub, wd))
    lnw = lnw_ref[...]
    lnb = lnb_ref[...]
    first = CONV_HALO - (CONV_WIDTH - 1)
    tiles = _CONV_ROWS // sub

    def chunk(c, carry):
        r0 = pl.multiple_of(c * _CONV_ROWS, _CONV_ROWS)
        acc = jnp.broadcast_to(wb_scr[CONV_WIDTH][None], (tiles, sub, wd))
        for j in range(CONV_WIDTH):
            res = (first + j) % sub
            win = u_scr[res, pl.ds(r0 + (first + j - res), _CONV_ROWS), :]
            acc = acc + wb_scr[j][None] * win.reshape(tiles, sub, wd)
        acc = acc.reshape(_CONV_ROWS, wd)
        mu = jnp.mean(acc, axis=-1, keepdims=True)
        d = acc - mu
        var = jnp.mean(d * d, axis=-1, keepdims=True)
        y = _silu(d * lax.rsqrt(var + EPS) * lnw + lnb)
        z = z_ref[pl.ds(r0, _CONV_ROWS), :].astype(F32)
        o_ref[pl.ds(r0, _CONV_ROWS), :] = (y * _silu(z)).astype(o_ref.dtype)
        return carry

    lax.fori_loop(0, ts // _CONV_ROWS, chunk, 0, unroll=4)


def _conv(proj3, conv_w, conv_b, ln_w, ln_b, layer):
    bsz, seq, _ = proj3.shape
    w = BRANCH_WIDTH
    ts = 512
    col = lambda cb: pl.BlockSpec((None, ts, w), lambda b, s: (b, s, cb))
    vec = lambda: pl.BlockSpec((None, 1, w), lambda b, s: (layer, 0, 0))
    depth = conv_w.shape[0]
    return pl.pallas_call(
        _conv_kernel,
        out_shape=jax.ShapeDtypeStruct((bsz, seq, w), BF16),
        grid=(bsz, seq // ts),
        in_specs=[
            col(_COL_CONV), col(_COL_CONV + 1), col(_COL_CONV + 2),
            pl.BlockSpec((None, CONV_WIDTH, w), lambda b, s: (layer, 0, 0)),
            vec(), vec(), vec(),
        ],
        out_specs=pl.BlockSpec((None, ts, w), lambda b, s: (b, s, 0)),
        scratch_shapes=[pltpu.VMEM((SUBLANES_V7X, CONV_HALO + ts, w), F32),
                        pltpu.VMEM((CONV_WIDTH + 1, SUBLANES_V7X, w), F32)],
        compiler_params=pltpu.CompilerParams(
            dimension_semantics=("arbitrary", "arbitrary"),
            vmem_limit_bytes=_mib(32)),
        name="conv_module",
    )(proj3, proj3, proj3, conv_w, conv_b.reshape(depth, 1, w), ln_w.reshape(depth, 1, w),
      ln_b.reshape(depth, 1, w))


def _sb_constants():
    n = SB_BLOCK
    u = (np.arange(n)[:, None] >= np.arange(n)[None, :]).astype(np.float32)
    return np.concatenate([u, np.ones((n, n), np.float32)], axis=1)


def _sb_kernel(q_ref, k_ref, v_ref, z_ref, u_ref, o_ref, q_scr, c_scr, o_scr, cmin_ref):
    n = SB_BLOCK
    pairs = q_ref.shape[1] // n
    prs = range(pairs)
    i = pl.program_id(1)
    lane = lax.broadcasted_iota(jnp.int32, (2 * n, n), 1)
    row = lax.broadcasted_iota(jnp.int32, (2 * n, n), 0)
    strict = lane < (row & (n - 1))
    head_a = lax.broadcasted_iota(jnp.int32, (n, n), 1) < SB_HEAD_DIM

    for p in prs:
        q = q_ref[:, p * n:(p + 1) * n].astype(F32) * (SB_HEAD_DIM ** -0.5)
        q_scr[p, 0:n, :] = jnp.where(head_a, q, 0.0).astype(BF16)
        q_scr[p, n:2 * n, :] = jnp.where(head_a, 0.0, q).astype(BF16)

    def visit(blocks, fresh):
        units = [(p, t) for p in prs for t in range(len(blocks))]
        kv = {}
        for t, (j, _) in enumerate(blocks):
            rows = pl.ds(pl.multiple_of(j * n, n), n)
            for p in prs:
                kv[p, t] = (k_ref[rows, p * n:(p + 1) * n], v_ref[rows, p * n:(p + 1) * n])
        logits = {u: _dot_nt(q_scr[u[0]], kv[u][0]) for u in units}
        sums = {}
        for u in units:
            p, t = u
            x = logits[u]
            drop = jnp.maximum(x, 0.0) + jnp.log(1.0 + jnp.exp2(jnp.abs(x) * (-LOG2E)))
            if blocks[t][1]:
                drop = jnp.where(strict, drop, 0.0)
            sums[u] = _dot(drop, u_ref[...])
        cmin = None
        pvs = []
        for p in prs:
            c = None if fresh else c_scr[p]
            pv = None
            for t in range(len(blocks)):
                r = sums[p, t]
                total = r[:, :n] if c is None else r[:, :n] + c
                w = jnp.exp(logits[p, t] - total)
                if blocks[t][1]:
                    w = jnp.where(strict, w, 0.0)
                term = _dot(w, kv[p, t][1])
                pv = term if pv is None else pv + term
                c = r[:, n:] if c is None else c + r[:, n:]
            c_scr[p] = c
            pvs.append(pv)
            cmin = c if cmin is None else jnp.minimum(cmin, c)
        for p in prs:
            o_scr[p] = pvs[p] if fresh else o_scr[p] + pvs[p]
        return jnp.min(cmin)

    @pl.when(i == 0)
    def _():
        cmin_ref[0] = visit([(i, True)], True)

    @pl.when(i == 1)
    def _():
        cmin_ref[0] = visit([(i, True), (i - 1, False)], True)

    @pl.when(i >= 2)
    def _():
        cmin_ref[0] = visit([(i, True), (i - 1, False), (i - 2, False)], True)

    def cond(carry):
        j, cmin = carry
        return jnp.logical_and(j >= 0, cmin < -SB_SKIP_LOG)

    def body(carry):
        j, _ = carry
        return j - 1, visit([(j, False)], False)

    lax.while_loop(cond, body, (i - 3, cmin_ref[0]))
    for p in range(pairs):
        o = jnp.where(head_a, o_scr[p, 0:n, :], o_scr[p, n:2 * n, :])
        cols = slice(p * n, (p + 1) * n)
        o_ref[:, cols] = (o * _silu(z_ref[:, cols].astype(F32))).astype(o_ref.dtype)


def _sb_attn(proj3):
    bsz, seq, _ = proj3.shape
    n = SB_BLOCK
    w = BRANCH_WIDTH
    base = _COL_SB
    u2 = _sb_constants()
    return pl.pallas_call(
        _sb_kernel,
        out_shape=jax.ShapeDtypeStruct((bsz, seq, w), BF16),
        grid=(bsz, seq // n),
        in_specs=[
            pl.BlockSpec((None, n, w), lambda b, i: (b, i, base)),
            pl.BlockSpec((None, seq, w), lambda b, i: (b, 0, base + 1)),
            pl.BlockSpec((None, seq, w), lambda b, i: (b, 0, base + 2)),
            pl.BlockSpec((None, n, w), lambda b, i: (b, i, base + 3)),
            pl.BlockSpec(u2.shape, lambda b, i: (0, 0)),
        ],
        out_specs=pl.BlockSpec((None, n, w), lambda b, i: (b, i, 0)),
        scratch_shapes=[pltpu.VMEM((w // n, 2 * n, n), BF16),
                        pltpu.VMEM((w // n, 2 * n, n), F32),
                        pltpu.VMEM((w // n, 2 * n, n), F32),
                        pltpu.SMEM((1,), F32)],
        compiler_params=pltpu.CompilerParams(
            dimension_semantics=("arbitrary", "arbitrary"),
            vmem_limit_bytes=_mib(40)),
        name="stick_breaking",
    )(proj3, proj3, proj3, proj3, jnp.asarray(u2, BF16))


def _merge_kernel(ya_ref, yb_ref, yc_ref, g0, g1, g2, g3, g4, g5, x_ref, gate_ref, wb_ref,
                  wo_ref, fnw_ref, o_ref, *, final):
    w = BRANCH_WIDTH
    ys = (ya_ref[...], yb_ref[...], yc_ref[...])
    gl = ((g0, g1), (g2, g3), (g4, g5))
    halves = []
    for half in range(2):
        acc = None
        for nb in range(N_BRANCH):
            br = jnp.dot(ys[nb], wb_ref[nb, :, half * w:(half + 1) * w],
                         preferred_element_type=F32)
            term = jax.nn.sigmoid(gl[nb][half][...].astype(F32)) * br
            acc = term if acc is None else acc + term
        halves.append(acc.astype(BF16))
    merged = jnp.concatenate(halves, axis=1)
    out = jnp.dot(merged, wo_ref[...], preferred_element_type=F32)
    xn = x_ref[...] + gate_ref[...] * out
    if final:
        ms = jnp.mean(xn * xn, axis=-1, keepdims=True)
        xn = xn * lax.rsqrt(ms + EPS) * fnw_ref[...]
    o_ref[...] = xn


def _merge(ya, yb, yc, proj, x2, mod5, wb_bf16, wo_bf16, final_norm_w, layer, seq, final):
    m, d = x2.shape
    w = BRANCH_WIDTH
    tm = 512
    per_seq = seq // tm
    gate0 = _COL_GATE
    ycol = lambda: pl.BlockSpec((tm, w), lambda i: (i, 0))
    gcol = lambda cb: pl.BlockSpec((tm, w), lambda i: (i, gate0 + cb))
    return pl.pallas_call(
        functools.partial(_merge_kernel, final=final),
        out_shape=jax.ShapeDtypeStruct((m, d), F32),
        grid=(m // tm,),
        in_specs=[
            ycol(), ycol(), ycol(),
            gcol(0), gcol(1), gcol(2), gcol(3), gcol(4), gcol(5),
            pl.BlockSpec((tm, d), lambda i: (i, 0)),
            pl.BlockSpec((None, None, None, 1, d), lambda i: (layer, i // per_seq, 2, 0, 0)),
            pl.BlockSpec((None, N_BRANCH, w, d), lambda i: (layer, 0, 0, 0)),
            pl.BlockSpec((None, d, d), lambda i: (layer, 0, 0)),
            pl.BlockSpec((1, d), lambda i: (0, 0)),
        ],
        out_specs=pl.BlockSpec((tm, d), lambda i: (i, 0)),
        compiler_params=pltpu.CompilerParams(
            dimension_semantics=("arbitrary",),
            vmem_limit_bytes=_mib(48)),
        name="merge_out",
    )(ya, yb, yc, proj, proj, proj, proj, proj, proj, x2, mod5, wb_bf16, wo_bf16,
      final_norm_w.reshape(1, d))


def kernel(x, c, ada_w, ada_b, norm_w, w_in, hgrn_lb, hgrn_norm_w, conv_w, conv_b, conv_ln_w,
           conv_ln_b, w_branch, w_out, final_norm_w):
    bsz, seq, d = x.shape
    depth = ada_w.shape[0]
    m = bsz * seq
    mod = _ada_mod(c, ada_w, ada_b)
    mod5 = mod.reshape(depth, bsz, 3, 1, d)
    w_in_b = w_in.astype(BF16)
    wb_b = w_branch.astype(BF16)
    wo_b = w_out.astype(BF16)
    x2 = x.reshape(m, d)
    for layer in range(depth):
        proj = _inproj(x2, mod5, norm_w, w_in_b, layer, seq)
        proj3 = proj.reshape(bsz, seq, proj.shape[-1])
        ya = _hgrn(proj3, hgrn_lb, hgrn_norm_w, layer)
        yb =_conv(proj3, conv_w, conv_b, conv_ln_w, conv_ln_b, layer)
        yc = _sb_attn(proj3)
        x2 = _merge(ya.reshape(m, -1), yb.reshape(m, -1), yc.reshape(m, -1), proj, x2, mod5,
                    wb_b, wo_b, final_norm_w, layer, seq, final=(layer == depth - 1))
    return x2.reshape(bsz, seq, d)
```

```python
import functools

import numpy as np
import jax
import jax.numpy as jnp
from jax import lax
from jax.experimental import pallas as pl
from jax.experimental.pallas import tpu as pltpu

F32 = jnp.float32
BF16 = jnp.bfloat16

SUBLANES_V7X = 8
VMEM_BYTES_V7X = 64 * 1024 * 1024

EPS = 1e-6
TINY = 1e-30
LOG2E = 1.4426950408889634
BRANCH_WIDTH = 512
HGRN_HEADS = 4
HGRN_HEAD_DIM = BRANCH_WIDTH // HGRN_HEADS
HGRN_CHUNK = 64
HGRN_LEVELS = 6
_HGRN_MXU_LEVELS = 3
HGRN_DIRECT_MAX_LOG2 = 120.0
CONV_WIDTH = 31
CONV_HALO = 32
SB_HEADS = 8
SB_HEAD_DIM = BRANCH_WIDTH // SB_HEADS
SB_BLOCK = 128
SB_SKIP_LOG = -104.0
N_BRANCH = 3


_COL_HGRN, _COL_CONV, _COL_SB, _COL_GATE = 0, 4, 7, 11


def _mib(n):
    assert int(n) * 1024 * 1024 <= VMEM_BYTES_V7X
    return int(n) * 1024 * 1024


def _silu(x):
    return x * jax.nn.sigmoid(x)


def _dot(a, b):
    return jnp.dot(a.astype(BF16), b.astype(BF16), preferred_element_type=F32)


def _dot_nt(a, b):
    return lax.dot_general(a.astype(BF16), b.astype(BF16), (((1,), (1,)), ((), ())),
                           preferred_element_type=F32)


def _dot_tn(a, b):
    return lax.dot_general(a.astype(BF16), b.astype(BF16), (((0,), (0,)), ((), ())),
                           preferred_element_type=F32)


def _split3(x):
    hi = x.astype(BF16)
    r = x - hi.astype(F32)
    mid = r.astype(BF16)
    lo = (r - mid.astype(F32)).astype(BF16)
    return hi, mid, lo


def _ada_kernel(c_ref, w_ref, b_ref, o_ref):
    c = c_ref[...]
    ca = _silu(c)
    w = w_ref[...]
    ch = ca.astype(BF16)
    cl = (ca - ch.astype(F32)).astype(BF16)
    wh = w.astype(BF16)
    wl = (w - wh.astype(F32)).astype(BF16)
    acc = (jnp.dot(ch, wh, preferred_element_type=F32)
           + jnp.dot(ch, wl, preferred_element_type=F32)
           + jnp.dot(cl, wh, preferred_element_type=F32))
    o_ref[...] = acc + b_ref[...]


def _ada_mod(c, ada_w, ada_b):
    depth, d, n = ada_w.shape
    bsz = c.shape[0]
    tn = n // 4
    return pl.pallas_call(
        _ada_kernel,
        out_shape=jax.ShapeDtypeStruct((depth, bsz, n), F32),
        grid=(depth, n // tn),
        in_specs=[
            pl.BlockSpec((bsz, d), lambda l, j: (0, 0)),
            pl.BlockSpec((None, d, tn), lambda l, j: (l, 0, j)),
            pl.BlockSpec((None, 1, tn), lambda l, j: (l, 0, j)),
        ],
        out_specs=pl.BlockSpec((None, bsz, tn), lambda l, j: (l, 0, j)),
        compiler_params=pltpu.CompilerParams(
            dimension_semantics=("arbitrary", "arbitrary"),
            vmem_limit_bytes=_mib(32)),
        name="ada_mod",
    )(c, ada_w, ada_b.reshape(depth, 1, n))


_INPROJ_NORM_ROWS = 256
_INPROJ_MM_ROWS = 1024


def _inproj_kernel(x_ref, shift_ref, scale_ref, nw_ref, w_ref, o_ref, h_scr):
    tm = x_ref.shape[0]

    @pl.when(pl.program_id(1) == 0)
    def _():
        gain = nw_ref[...] * (1.0 + scale_ref[...])
        shift = shift_ref[...]

        def body(i, carry):
            r0 = pl.multiple_of(i * _INPROJ_NORM_ROWS, _INPROJ_NORM_ROWS)
            x = x_ref[pl.ds(r0, _INPROJ_NORM_ROWS), :]
            ms = jnp.mean(x * x, axis=-1, keepdims=True)
            h = (x * lax.rsqrt(ms + EPS)) * gain + shift
            h_scr[pl.ds(r0, _INPROJ_NORM_ROWS), :] = h.astype(BF16)
            return carry

        lax.fori_loop(0, tm // _INPROJ_NORM_ROWS, body, 0)

    def mm(i, carry):
        r0 = pl.multiple_of(i * _INPROJ_MM_ROWS, _INPROJ_MM_ROWS)
        o_ref[pl.ds(r0, _INPROJ_MM_ROWS), :] = jnp.dot(
            h_scr[pl.ds(r0, _INPROJ_MM_ROWS), :], w_ref[...],
            preferred_element_type=F32).astype(o_ref.dtype)
        return carry

    lax.fori_loop(0, tm // _INPROJ_MM_ROWS, mm, 0)


def _inproj(x2, mod5, norm_w, w_in_bf16, layer, seq):
    m, d = x2.shape
    n = w_in_bf16.shape[-1]
    tm = seq
    tn = n // 4
    per_seq = seq // tm
    return pl.pallas_call(
        _inproj_kernel,
        out_shape=jax.ShapeDtypeStruct((m, n), BF16),
        grid=(m // tm, n // tn),
        in_specs=[
            pl.BlockSpec((tm, d), lambda i, j: (i, 0)),
            pl.BlockSpec((None, None, None, 1, d), lambda i, j: (layer, i // per_seq, 0, 0, 0)),
            pl.BlockSpec((None, None, None, 1, d), lambda i, j: (layer, i // per_seq, 1, 0, 0)),
            pl.BlockSpec((None, 1, d), lambda i, j: (layer, 0, 0)),
            pl.BlockSpec((None, d, tn), lambda i, j: (layer, 0, j)),
        ],
        out_specs=pl.BlockSpec((tm, tn), lambda i, j: (i, j)),
        scratch_shapes=[pltpu.VMEM((tm, d), BF16)],
        compiler_params=pltpu.CompilerParams(
            dimension_semantics=("arbitrary", "arbitrary"),
            vmem_limit_bytes=_mib(56)),
        name="inproj",
    )(x2, mod5, mod5, norm_w.reshape(norm_w.shape[0], 1, d), w_in_bf16)


def _hgrn_constants():
    L = HGRN_CHUNK
    tri = np.tril(np.ones((L, L), np.float32))
    blocks = [tri]
    r = np.arange(L)
    for l in range(_HGRN_MXU_LEVELS):
        h = 1 << l
        p = (r // (2 * h)) * (2 * h) + h - 1
        sign = np.where(r > p, 1.0, -1.0).astype(np.float32)[:, None]
        blocks.append(sign * (tri - tri[p]))
    c_all = np.concatenate(blocks, axis=0)
    c_all = np.concatenate([c_all, c_all, c_all], axis=1)
    t = r[:, None]
    s = r[None, :]
    x = t ^ s
    lev = np.where(x > 0, np.floor(np.log2(np.maximum(x, 1))).astype(np.int32), HGRN_LEVELS)
    lev = np.where(s > t, -1, lev).astype(np.int32)
    return c_all, lev


def _hgrn_kernel(q_ref, f_ref, i_ref, z_ref, lb_ref, nw_ref, c_ref, lev_ref, ind_ref, o_ref,
                 st_ref, g_scr, k_scr, *, layer):
    L = HGRN_CHUNK
    K = HGRN_HEAD_DIM
    n_chunks = q_ref.shape[0] // L

    @pl.when(pl.program_id(1) == 0)
    def _():
        st_ref[...] = jnp.zeros_like(st_ref)

    lb_all = lb_ref[...]
    lb_exp = jnp.exp(lb_all - jnp.max(lb_all, axis=0, keepdims=True))
    lb_soft = lb_exp / jnp.sum(lb_exp, axis=0, keepdims=True)
    lower = jnp.zeros((1, lb_all.shape[1]), F32)
    for l in range(1, layer + 1):
        lower = lower + lb_soft[l:l + 1, :]
    one_m_lower = 1.0 - lower
    nw = nw_ref[...]

    sig_all = jax.nn.sigmoid(f_ref[...].astype(F32))
    f_all = lower + one_m_lower * sig_all
    k_scr[...] = one_m_lower * (1.0 - sig_all)
    g_all = jnp.log(jnp.maximum(f_all, TINY)) * LOG2E
    g_scr[...] = g_all
    totals = jnp.dot(ind_ref[...], g_all.astype(BF16), preferred_element_type=F32)
    n_ind = ind_ref.shape[0] // 2
    worst = -jnp.min(totals[0:n_ind])
    worst_half = -jnp.min(totals[n_ind:])

    def body(c, carry, path):
        direct = path != "general"
        r0 = pl.multiple_of(c * L, L)
        rows = pl.ds(r0, L)
        q = _silu(q_ref[rows, :].astype(F32)) * (K ** -0.5)
        g = g_scr[rows, :]
        k = k_scr[rows, :]
        v = i_ref[rows, :]
        z = z_ref[rows, :].astype(F32)
        cmat = c_ref[0:L, :] if direct else c_ref[...]
        dall = jnp.dot(cmat, jnp.concatenate(_split3(g), axis=0),
                       preferred_element_type=F32)
        lev = lev_ref[...]
        heads = range(HGRN_HEADS)
        hcols = [slice(h * K, (h + 1) * K) for h in heads]

        if direct:
            pair, o_inter, upd, decay = [], [], [], []
            for h in heads:
                cols = hcols[h]
                qh, kh, vh = q[:, cols], k[:, cols], v[:, cols]
                b = dall[:, cols]
                b_last = b[L - 1:L, :]
                if path == "direct":
                    qe = qh * jnp.exp2(b)
                    pair.append(_dot_nt(qe, kh * jnp.exp2(-b)))
                else:
                    half = L // 2
                    b_mid = b[half - 1:half, :]
                    d = jnp.concatenate([b[0:half, :], b[half:L, :] - b_mid], axis=0)
                    qf = qh * jnp.exp2(d)
                    qe = jnp.concatenate([qf[0:half, :], qf[half:L, :] * jnp.exp2(b_mid)], axis=0)
                    same = _dot_nt(qf, kh * jnp.exp2(-d))
                    cross = _dot_nt(qf, kh * jnp.exp2(b_mid - b))
                    pair.append(jnp.where(lev == HGRN_LEVELS - 1, cross, same))
                o_inter.append(_dot_nt(qe, st_ref[h]))
                upd.append(_dot_tn(vh, kh * jnp.exp2(b_last - b)))
                decay.append(jnp.exp2(b_last))
            outs = [o_inter[h] + _dot(jnp.where(lev >= 0, pair[h], 0.0), v[:, hcols[h]])
                    for h in heads]
            for h in heads:
                st_ref[h] = st_ref[h] * decay[h] + upd[h]
                o = outs[h]
                on = o * lax.rsqrt(jnp.mean(o * o, axis=-1, keepdims=True) + EPS) * nw
                o_ref[rows, hcols[h]] = (on * _silu(z[:, hcols[h]])).astype(o_ref.dtype)
            return carry

        def boundary_gap(b, l, cols):
            if l < _HGRN_MXU_LEVELS:
                return dall[(l + 1) * L:(l + 2) * L, cols]
            hw = 1 << l
            parts = []
            for s in range(0, L, 2 * hw):
                bp = b[s + hw - 1:s + hw, :]
                parts += [bp - b[s:s + hw, :], b[s + hw:s + 2 * hw, :] - bp]
            return jnp.concatenate(parts, axis=0)

        pair, o_inter, upd, decay = [], [], [], []
        for h in heads:
            cols = hcols[h]
            qh, kh, vh = q[:, cols], k[:, cols], v[:, cols]
            b = dall[0:L, cols]
            b_last = b[L - 1:L, :]
            o_inter.append(_dot_nt(qh * jnp.exp2(b), st_ref[h]))
            upd.append(_dot_tn(vh, kh * jnp.exp2(b_last - b)))
            decay.append(jnp.exp2(b_last))
            prods = [_dot_nt(qh, kh)]
            for l in range(HGRN_LEVELS):
                fac = jnp.exp2(boundary_gap(b, l, cols))
                prods.append(_dot_nt(qh * fac, kh * fac))
            pair.append(prods)
        outs = []
        for h in heads:
            a = jnp.where(lev == HGRN_LEVELS, pair[h][0], 0.0)
            for l in range(HGRN_LEVELS):
                a = jnp.where(lev == l, pair[h][l + 1], a)
            outs.append(o_inter[h] + _dot(a, v[:, hcols[h]]))
        for h in heads:
            st_ref[h] = st_ref[h] * decay[h] + upd[h]
            o = outs[h]
            on = o * lax.rsqrt(jnp.mean(o * o, axis=-1, keepdims=True) + EPS) * nw
            o_ref[rows, hcols[h]] = (on * _silu(z[:, hcols[h]])).astype(o_ref.dtype)
        return carry

    direct_ok = worst < HGRN_DIRECT_MAX_LOG2
    halves_ok = jnp.logical_and(jnp.logical_not(direct_ok), worst_half < HGRN_DIRECT_MAX_LOG2)

    @pl.when(direct_ok)
    def _():
        lax.fori_loop(0, n_chunks, functools.partial(body, path="direct"), 0, unroll=8)

    @pl.when(halves_ok)
    def _():
        lax.fori_loop(0, n_chunks, functools.partial(body, path="halves"), 0, unroll=8)

    @pl.when(jnp.logical_not(jnp.logical_or(direct_ok, halves_ok)))
    def _():
        lax.fori_loop(0, n_chunks, functools.partial(body, path="general"), 0, unroll=4)


def _hgrn(proj3, hgrn_lb, hgrn_norm_w, layer):
    bsz, seq, _ = proj3.shape
    w = BRANCH_WIDTH
    tb = 512
    c_all, lev = _hgrn_constants()
    half = HGRN_CHUNK // 2
    n_ind = tb // half
    ind = np.zeros((2 * n_ind, tb), np.float32)
    for c in range(tb // HGRN_CHUNK):
        ind[c, c * HGRN_CHUNK:(c + 1) * HGRN_CHUNK] = 1.0
    for c in range(n_ind):
        ind[n_ind + c, c * half:(c + 1) * half] = 1.0
    col = lambda cb: pl.BlockSpec((None, tb, w), lambda b, s: (b, s, cb))
    return pl.pallas_call(
        functools.partial(_hgrn_kernel, layer=layer),
        out_shape=jax.ShapeDtypeStruct((bsz, seq, w), BF16),
        grid=(bsz, seq // tb),
        in_specs=[
            col(_COL_HGRN), col(_COL_HGRN + 1), col(_COL_HGRN + 2), col(_COL_HGRN + 3),
            pl.BlockSpec(hgrn_lb.shape, lambda b, s: (0, 0)),
            pl.BlockSpec((None, 1, HGRN_HEAD_DIM), lambda b, s: (layer, 0, 0)),
            pl.BlockSpec(c_all.shape, lambda b, s: (0, 0)),
            pl.BlockSpec(lev.shape, lambda b, s: (0, 0)),
            pl.BlockSpec(ind.shape, lambda b, s: (0, 0)),
        ],
        out_specs=pl.BlockSpec((None, tb, w), lambda b, s: (b, s, 0)),
        scratch_shapes=[pltpu.VMEM((HGRN_HEADS, HGRN_HEAD_DIM, HGRN_HEAD_DIM), F32),
                        pltpu.VMEM((tb, w), F32),
                        pltpu.VMEM((tb, w), F32)],
        compiler_params=pltpu.CompilerParams(
            dimension_semantics=("arbitrary", "arbitrary"),
            vmem_limit_bytes=_mib(32)),
        name="hgrn2",
    )(proj3, proj3, proj3, proj3, hgrn_lb,
      hgrn_norm_w.reshape(hgrn_norm_w.shape[0], 1, HGRN_HEAD_DIM),
      jnp.asarray(c_all, BF16), jnp.asarray(lev), jnp.asarray(ind, BF16))


_CONV_ROWS = 32
_CONV_GLU_ROWS = 64


def _conv_kernel(a_ref, g_ref, z_ref, w_ref, b_ref, lnw_ref, lnb_ref, o_ref, u_scr, wb_scr):
    ts = a_ref.shape[0]
    sub = SUBLANES_V7X

    @pl.when(pl.program_id(1) == 0)
    def _():
        u_scr[:, 0:CONV_HALO, :] = jnp.zeros((sub, CONV_HALO, u_scr.shape[2]), F32)

    @pl.when(pl.program_id(1) > 0)
    def _():
        for res in range(sub):
            u_scr[res, 0:CONV_HALO - res, :] = u_scr[res, ts:ts + CONV_HALO - res, :]

    for c in range(ts // _CONV_GLU_ROWS):
        r0 = c * _CONV_GLU_ROWS
        u = (a_ref[r0:r0 + _CONV_GLU_ROWS, :].astype(F32)
             * jax.nn.sigmoid(g_ref[r0:r0 + _CONV_GLU_ROWS, :].astype(F32)))
        for res in range(sub):
            u_scr[res, CONV_HALO - res + r0:CONV_HALO - res + r0 + _CONV_GLU_ROWS, :] = u

    wd = w_ref.shape[1]
    for j in range(CONV_WIDTH):
        wb_scr[j] = jnp.broadcast_to(w_ref[j:j + 1, :], (sub, wd))
    wb_scr[CONV_WIDTH] = jnp.broadcast_to(b_ref[...], (sub, wd))
    lnw = lnw_ref[...]
    lnb = lnb_ref[...]
    first = CONV_HALO - (CONV_WIDTH - 1)
    tiles = _CONV_ROWS // sub

    def chunk(c, carry):
        r0 = pl.multiple_of(c * _CONV_ROWS, _CONV_ROWS)
        acc = jnp.broadcast_to(wb_scr[CONV_WIDTH][None], (tiles, sub, wd))
        for j in range(CONV_WIDTH):
            res = (first + j) % sub
            win = u_scr[res, pl.ds(r0 + (first + j - res), _CONV_ROWS), :]
            acc = acc + wb_scr[j][None] * win.reshape(tiles, sub, wd)
        acc = acc.reshape(_CONV_ROWS, wd)
        mu = jnp.mean(acc, axis=-1, keepdims=True)
        d = acc - mu
        var = jnp.mean(d * d, axis=-1, keepdims=True)
        y = _silu(d * lax.rsqrt(var + EPS) * lnw + lnb)
        z = z_ref[pl.ds(r0, _CONV_ROWS), :].astype(F32)
        o_ref[pl.ds(r0, _CONV_ROWS), :] = (y * _silu(z)).astype(o_ref.dtype)
        return carry

    lax.fori_loop(0, ts // _CONV_ROWS, chunk, 0, unroll=8)


def _conv(proj3, conv_w, conv_b, ln_w, ln_b, layer):
    bsz, seq, _ = proj3.shape
    w = BRANCH_WIDTH
    ts = 512
    col = lambda cb: pl.BlockSpec((None, ts, w), lambda b, s: (b, s, cb))
    vec = lambda: pl.BlockSpec((None, 1, w), lambda b, s: (layer, 0, 0))
    depth = conv_w.shape[0]
    return pl.pallas_call(
        _conv_kernel,
        out_shape=jax.ShapeDtypeStruct((bsz, seq, w), BF16),
        grid=(bsz, seq // ts),
        in_specs=[
            col(_COL_CONV), col(_COL_CONV + 1), col(_COL_CONV + 2),
            pl.BlockSpec((None, CONV_WIDTH, w), lambda b, s: (layer, 0, 0)),
            vec(), vec(), vec(),
        ],
        out_specs=pl.BlockSpec((None, ts, w), lambda b, s: (b, s, 0)),
        scratch_shapes=[pltpu.VMEM((SUBLANES_V7X, CONV_HALO + ts, w), F32),
                        pltpu.VMEM((CONV_WIDTH + 1, SUBLANES_V7X, w), F32)],
        compiler_params=pltpu.CompilerParams(
            dimension_semantics=("arbitrary", "arbitrary"),
            vmem_limit_bytes=_mib(32)),
        name="conv_module",
    )(proj3, proj3, proj3, conv_w, conv_b.reshape(depth, 1, w), ln_w.reshape(depth, 1, w),
      ln_b.reshape(depth, 1, w))


def _sb_constants():
    n = SB_BLOCK
    u = (np.arange(n)[:, None] >= np.arange(n)[None, :]).astype(np.float32)
    return np.concatenate([u, np.ones((n, n), np.float32)], axis=1)


def _sb_kernel(q_ref, k_ref, v_ref, z_ref, u_ref, o_ref, q_scr, c_scr, o_scr, cmin_ref):
    n = SB_BLOCK
    pairs = q_ref.shape[1] // n
    prs = range(pairs)
    i = pl.program_id(1)
    lane = lax.broadcasted_iota(jnp.int32, (2 * n, n), 1)
    row = lax.broadcasted_iota(jnp.int32, (2 * n, n), 0)
    strict = lane < (row & (n - 1))
    head_a = lax.broadcasted_iota(jnp.int32, (n, n), 1) < SB_HEAD_DIM

    for p in prs:
        q = q_ref[:, p * n:(p + 1) * n].astype(F32) * (SB_HEAD_DIM ** -0.5)
        q_scr[p, 0:n, :] = jnp.where(head_a, q, 0.0).astype(BF16)
        q_scr[p, n:2 * n, :] = jnp.where(head_a, 0.0, q).astype(BF16)

    def visit(blocks, fresh):
        units = [(p, t) for p in prs for t in range(len(blocks))]
        kv = {}
        for t, (j, _) in enumerate(blocks):
            rows = pl.ds(pl.multiple_of(j * n, n), n)
            for p in prs:
                kv[p, t] = (k_ref[rows, p * n:(p + 1) * n], v_ref[rows, p * n:(p + 1) * n])
        logits = {u: _dot_nt(q_scr[u[0]], kv[u][0]) for u in units}
        sums = {}
        for u in units:
            p, t = u
            x = logits[u]
            drop = jnp.maximum(x, 0.0) + jnp.log(1.0 + jnp.exp2(jnp.abs(x) * (-LOG2E)))
            if blocks[t][1]:
                drop = jnp.where(strict, drop, 0.0)
            sums[u] = _dot(drop, u_ref[...])
        cmin = None
        pvs = []
        for p in prs:
            c = None if fresh else c_scr[p]
            pv = None
            for t in range(len(blocks)):
                r = sums[p, t]
                total = r[:, :n] if c is None else r[:, :n] + c
                w = jnp.exp(logits[p, t] - total)
                if blocks[t][1]:
                    w = jnp.where(strict, w, 0.0)
                term = _dot(w, kv[p, t][1])
                pv = term if pv is None else pv + term
                c = r[:, n:] if c is None else c + r[:, n:]
            c_scr[p] = c
            pvs.append(pv)
            cmin = c if cmin is None else jnp.minimum(cmin, c)
        for p in prs:
            o_scr[p] = pvs[p] if fresh else o_scr[p] + pvs[p]
        return jnp.min(cmin)

    @pl.when(i == 0)
    def _():
        cmin_ref[0] = visit([(i, True)], True)

    @pl.when(i == 1)
    def _():
        cmin_ref[0] = visit([(i, True), (i - 1, False)], True)

    @pl.when(i >= 2)
    def _():
        cmin_ref[0] = visit([(i, True), (i - 1, False), (i - 2, False)], True)

    def cond(carry):
        j, cmin = carry
        return jnp.logical_and(j >= 0, cmin < -SB_SKIP_LOG)

    def body(carry):
        j, _ = carry
        return j - 1, visit([(j, False)], False)

    lax.while_loop(cond, body, (i - 3, cmin_ref[0]))
    for p in range(pairs):
        o = jnp.where(head_a, o_scr[p, 0:n, :], o_scr[p, n:2 * n, :])
        cols = slice(p * n, (p + 1) * n)
        o_ref[:, cols] = (o * _silu(z_ref[:, cols].astype(F32))).astype(o_ref.dtype)


def _sb_attn(proj3):
    bsz, seq, _ = proj3.shape
    n = SB_BLOCK
    w = BRANCH_WIDTH
    base = _COL_SB
    u2 = _sb_constants()
    return pl.pallas_call(
        _sb_kernel,
        out_shape=jax.ShapeDtypeStruct((bsz, seq, w), BF16),
        grid=(bsz, seq // n),
        in_specs=[
            pl.BlockSpec((None, n, w), lambda b, i: (b, i, base)),
            pl.BlockSpec((None, seq, w), lambda b, i: (b, 0, base + 1)),
            pl.BlockSpec((None, seq, w), lambda b, i: (b, 0, base + 2)),
            pl.BlockSpec((None, n, w), lambda b, i: (b, i, base + 3)),
            pl.BlockSpec(u2.shape, lambda b, i: (0, 0)),
        ],
        out_specs=pl.BlockSpec((None, n, w), lambda b, i: (b, i, 0)),
        scratch_shapes=[pltpu.VMEM((w // n, 2 * n, n), BF16),
                        pltpu.VMEM((w // n, 2 * n, n), F32),
                        pltpu.VMEM((w // n, 2 * n, n), F32),
                        pltpu.SMEM((1,), F32)],
        compiler_params=pltpu.CompilerParams(
            dimension_semantics=("arbitrary", "arbitrary"),
            vmem_limit_bytes=_mib(40)),
        name="stick_breaking",
    )(proj3, proj3, proj3, proj3, jnp.asarray(u2, BF16))


def _merge_kernel(ya_ref, yb_ref, yc_ref, g0, g1, g2, g3, g4, g5, x_ref, gate_ref, wb_ref,
                  wo_ref, fnw_ref, o_ref, *, final):
    w = BRANCH_WIDTH
    ys = (ya_ref[...], yb_ref[...], yc_ref[...])
    gl = ((g0, g1), (g2, g3), (g4, g5))
    halves = []
    for half in range(2):
        acc = None
        for nb in range(N_BRANCH):
            br = jnp.dot(ys[nb], wb_ref[nb, :, half * w:(half + 1) * w],
                         preferred_element_type=F32)
            term = jax.nn.sigmoid(gl[nb][half][...].astype(F32)) * br
            acc = term if acc is None else acc + term
        halves.append(acc.astype(BF16))
    merged = jnp.concatenate(halves, axis=1)
    out = jnp.dot(merged, wo_ref[...], preferred_element_type=F32)
    xn = x_ref[...] + gate_ref[...] * out
    if final:
        ms = jnp.mean(xn * xn, axis=-1, keepdims=True)
        xn = xn * lax.rsqrt(ms + EPS) * fnw_ref[...]
    o_ref[...] = xn


def _merge(ya, yb, yc, proj, x2, mod5, wb_bf16, wo_bf16, final_norm_w, layer, seq, final):
    m, d = x2.shape
    w = BRANCH_WIDTH
    tm = 512
    per_seq = seq // tm
    gate0 = _COL_GATE
    ycol = lambda: pl.BlockSpec((tm, w), lambda i: (i, 0))
    gcol = lambda cb: pl.BlockSpec((tm, w), lambda i: (i, gate0 + cb))
    return pl.pallas_call(
        functools.partial(_merge_kernel, final=final),
        out_shape=jax.ShapeDtypeStruct((m, d), F32),
        grid=(m // tm,),
        in_specs=[
            ycol(), ycol(), ycol(),
            gcol(0), gcol(1), gcol(2), gcol(3), gcol(4), gcol(5),
            pl.BlockSpec((tm, d), lambda i: (i, 0)),
            pl.BlockSpec((None, None, None, 1, d), lambda i: (layer, i // per_seq, 2, 0, 0)),
            pl.BlockSpec((None, N_BRANCH, w, d), lambda i: (layer, 0, 0, 0)),
            pl.BlockSpec((None, d, d), lambda i: (layer, 0, 0)),
            pl.BlockSpec((1, d), lambda i: (0, 0)),
        ],
        out_specs=pl.BlockSpec((tm, d), lambda i: (i, 0)),
        compiler_params=pltpu.CompilerParams(
            dimension_semantics=("arbitrary",),
            vmem_limit_bytes=_mib(48)),
        name="merge_out",
    )(ya, yb, yc, proj, proj, proj, proj, proj, proj, x2, mod5, wb_bf16, wo_bf16,
      final_norm_w.reshape(1, d))


def kernel(x, c, ada_w, ada_b, norm_w, w_in, hgrn_lb, hgrn_norm_w, conv_w, conv_b, conv_ln_w,
           conv_ln_b, w_branch, w_out, final_norm_w):
    bsz, seq, d = x.shape
    depth = ada_w.shape[0]
    m = bsz * seq
    mod = _ada_mod(c, ada_w, ada_b)
    mod5 = mod.reshape(depth, bsz, 3, 1, d)
    w_in_b = w_in.astype(BF16)
    wb_b = w_branch.astype(BF16)
    wo_b = w_out.astype(BF16)
    x2 = x.reshape(m, d)
    for layer in range(depth):
        proj = _inproj(x2, mod5, norm_w, w_in_b, layer, seq)
        proj3 = proj.reshape(bsz, seq, proj.shape[-1])
        ya = _hgrn(proj3, hgrn_lb, hgrn_norm_w, layer)
        yb =_conv(proj3, conv_w, conv_b, conv_ln_w, conv_ln_b, layer)
        yc = _sb_attn(proj3)
        x2 = _merge(ya.reshape(m, -1), yb.reshape(m, -1), yc.reshape(m, -1), proj, x2, mod5,
                    wb_b, wo_b, final_norm_w, layer, seq, final=(layer == depth - 1))
    return x2.reshape(bsz, seq, d)
```

```python
import functools

import numpy as np
import jax
import jax.numpy as jnp
from jax import lax
from jax.experimental import pallas as pl
from jax.experimental.pallas import tpu as pltpu

F32 = jnp.float32
BF16 = jnp.bfloat16

SUBLANES_V7X = 8
VMEM_BYTES_V7X = 64 * 1024 * 1024

EPS = 1e-6
TINY = 1e-30
LOG2E = 1.4426950408889634
BRANCH_WIDTH = 512
HGRN_HEADS = 4
HGRN_HEAD_DIM = BRANCH_WIDTH // HGRN_HEADS
HGRN_CHUNK = 64
HGRN_LEVELS = 6
_HGRN_MXU_LEVELS = 3
HGRN_DIRECT_MAX_LOG2 = 120.0
CONV_WIDTH = 31
CONV_HALO = 32
SB_HEADS = 8
SB_HEAD_DIM = BRANCH_WIDTH // SB_HEADS
SB_BLOCK = 128
SB_SKIP_LOG = -104.0
N_BRANCH = 3


_COL_HGRN, _COL_CONV, _COL_SB, _COL_GATE = 0, 4, 7, 11


def _mib(n):
    assert int(n) * 1024 * 1024 <= VMEM_BYTES_V7X
    return int(n) * 1024 * 1024


def _silu(x):
    return x * jax.nn.sigmoid(x)


def _dot(a, b):
    return jnp.dot(a.astype(BF16), b.astype(BF16), preferred_element_type=F32)


def _dot_nt(a, b):
    return lax.dot_general(a.astype(BF16), b.astype(BF16), (((1,), (1,)), ((), ())),
                           preferred_element_type=F32)


def _dot_tn(a, b):
    return lax.dot_general(a.astype(BF16), b.astype(BF16), (((0,), (0,)), ((), ())),
                           preferred_element_type=F32)


def _split3(x):
    hi = x.astype(BF16)
    r = x - hi.astype(F32)
    mid = r.astype(BF16)
    lo = (r - mid.astype(F32)).astype(BF16)
    return hi, mid, lo


def _ada_kernel(c_ref, w_ref, b_ref, o_ref):
    c = c_ref[...]
    ca = _silu(c)
    w = w_ref[...]
    ch = ca.astype(BF16)
    cl = (ca - ch.astype(F32)).astype(BF16)
    wh = w.astype(BF16)
    wl = (w - wh.astype(F32)).astype(BF16)
    acc = (jnp.dot(ch, wh, preferred_element_type=F32)
           + jnp.dot(ch, wl, preferred_element_type=F32)
           + jnp.dot(cl, wh, preferred_element_type=F32))
    o_ref[...] = acc + b_ref[...]


def _ada_mod(c, ada_w, ada_b):
    depth, d, n = ada_w.shape
    bsz = c.shape[0]
    tn = n // 4
    return pl.pallas_call(
        _ada_kernel,
        out_shape=jax.ShapeDtypeStruct((depth, bsz, n), F32),
        grid=(depth, n // tn),
        in_specs=[
            pl.BlockSpec((bsz, d), lambda l, j: (0, 0)),
            pl.BlockSpec((None, d, tn), lambda l, j: (l, 0, j)),
            pl.BlockSpec((None, 1, tn), lambda l, j: (l, 0, j)),
        ],
        out_specs=pl.BlockSpec((None, bsz, tn), lambda l, j: (l, 0, j)),
        compiler_params=pltpu.CompilerParams(
            dimension_semantics=("arbitrary", "arbitrary"),
            vmem_limit_bytes=_mib(32)),
        name="ada_mod",
    )(c, ada_w, ada_b.reshape(depth, 1, n))


_INPROJ_NORM_ROWS = 256
_INPROJ_MM_ROWS = 1024


def _inproj_kernel(x_ref, shift_ref, scale_ref, nw_ref, w_ref, o_ref, h_scr):
    tm = x_ref.shape[0]

    @pl.when(pl.program_id(1) == 0)
    def _():
        gain = nw_ref[...] * (1.0 + scale_ref[...])
        shift = shift_ref[...]

        def body(i, carry):
            r0 = pl.multiple_of(i * _INPROJ_NORM_ROWS, _INPROJ_NORM_ROWS)
            x = x_ref[pl.ds(r0, _INPROJ_NORM_ROWS), :]
            ms = jnp.mean(x * x, axis=-1, keepdims=True)
            h = (x * lax.rsqrt(ms + EPS)) * gain + shift
            h_scr[pl.ds(r0, _INPROJ_NORM_ROWS), :] = h.astype(BF16)
            return carry

        lax.fori_loop(0, tm // _INPROJ_NORM_ROWS, body, 0)

    def mm(i, carry):
        r0 = pl.multiple_of(i * _INPROJ_MM_ROWS, _INPROJ_MM_ROWS)
        o_ref[pl.ds(r0, _INPROJ_MM_ROWS), :] = jnp.dot(
            h_scr[pl.ds(r0, _INPROJ_MM_ROWS), :], w_ref[...],
            preferred_element_type=F32).astype(o_ref.dtype)
        return carry

    lax.fori_loop(0, tm // _INPROJ_MM_ROWS, mm, 0)


def _inproj(x2, mod5, norm_w, w_in_bf16, layer, seq):
    m, d = x2.shape
    n = w_in_bf16.shape[-1]
    tm = seq
    tn = n // 4
    per_seq = seq // tm
    return pl.pallas_call(
        _inproj_kernel,
        out_shape=jax.ShapeDtypeStruct((m, n), BF16),
        grid=(m // tm, n // tn),
        in_specs=[
            pl.BlockSpec((tm, d), lambda i, j: (i, 0)),
            pl.BlockSpec((None, None, None, 1, d), lambda i, j: (layer, i // per_seq, 0, 0, 0)),
            pl.BlockSpec((None, None, None, 1, d), lambda i, j: (layer, i // per_seq, 1, 0, 0)),
            pl.BlockSpec((None, 1, d), lambda i, j: (layer, 0, 0)),
            pl.BlockSpec((None, d, tn), lambda i, j: (layer, 0, j)),
        ],
        out_specs=pl.BlockSpec((tm, tn), lambda i, j: (i, j)),
        scratch_shapes=[pltpu.VMEM((tm, d), BF16)],
        compiler_params=pltpu.CompilerParams(
            dimension_semantics=("arbitrary", "arbitrary"),
            vmem_limit_bytes=_mib(56)),
        name="inproj",
    )(x2, mod5, mod5, norm_w.reshape(norm_w.shape[0], 1, d), w_in_bf16)


def _hgrn_constants():
    L = HGRN_CHUNK
    tri = np.tril(np.ones((L, L), np.float32))
    blocks = [tri]
    r = np.arange(L)
    for l in range(_HGRN_MXU_LEVELS):
        h = 1 << l
        p = (r // (2 * h)) * (2 * h) + h - 1
        sign = np.where(r > p, 1.0, -1.0).astype(np.float32)[:, None]
        blocks.append(sign * (tri - tri[p]))
    c_all = np.concatenate(blocks, axis=0)
    c_all = np.concatenate([c_all, c_all, c_all], axis=1)
    t = r[:, None]
    s = r[None, :]
    x = t ^ s
    lev = np.where(x > 0, np.floor(np.log2(np.maximum(x, 1))).astype(np.int32), HGRN_LEVELS)
    lev = np.where(s > t, -1, lev).astype(np.int32)
    return c_all, lev


def _hgrn_kernel(q_ref, f_ref, i_ref, z_ref, lb_ref, nw_ref, c_ref, lev_ref, ind_ref, o_ref,
                 st_ref, g_scr, k_scr, *, layer):
    L = HGRN_CHUNK
    K = HGRN_HEAD_DIM
    n_chunks = q_ref.shape[0] // L

    @pl.when(pl.program_id(1) == 0)
    def _():
        st_ref[...] = jnp.zeros_like(st_ref)

    lb_all = lb_ref[...]
    lb_exp = jnp.exp(lb_all - jnp.max(lb_all, axis=0, keepdims=True))
    lb_soft = lb_exp / jnp.sum(lb_exp, axis=0, keepdims=True)
    lower = jnp.zeros((1, lb_all.shape[1]), F32)
    for l in range(1, layer + 1):
        lower = lower + lb_soft[l:l + 1, :]
    one_m_lower = 1.0 - lower
    nw = nw_ref[...]

    sig_all = jax.nn.sigmoid(f_ref[...].astype(F32))
    f_all = lower + one_m_lower * sig_all
    k_scr[...] = one_m_lower * (1.0 - sig_all)
    g_all = jnp.log(jnp.maximum(f_all, TINY)) * LOG2E
    g_scr[...] = g_all
    totals = jnp.dot(ind_ref[...], g_all.astype(BF16), preferred_element_type=F32)
    n_ind = ind_ref.shape[0] // 2
    worst = -jnp.min(totals[0:n_ind])
    worst_half = -jnp.min(totals[n_ind:])

    def body(c, carry, path):
        direct = path != "general"
        r0 = pl.multiple_of(c * L, L)
        rows = pl.ds(r0, L)
        q = _silu(q_ref[rows, :].astype(F32)) * (K ** -0.5)
        g = g_scr[rows, :]
        k = k_scr[rows, :]
        v = i_ref[rows, :]
        z = z_ref[rows, :].astype(F32)
        cmat = c_ref[0:L, :] if direct else c_ref[...]
        dall = jnp.dot(cmat, jnp.concatenate(_split3(g), axis=0),
                       preferred_element_type=F32)
        lev = lev_ref[...]
        heads = range(HGRN_HEADS)
        hcols = [slice(h * K, (h + 1) * K) for h in heads]

        if direct:
            pair, o_inter, upd, decay = [], [], [], []
            for h in heads:
                cols = hcols[h]
                qh, kh, vh = q[:, cols], k[:, cols], v[:, cols]
                b = dall[:, cols]
                b_last = b[L - 1:L, :]
                if path == "direct":
                    qe = qh * jnp.exp2(b)
                    pair.append(_dot_nt(qe, kh * jnp.exp2(-b)))
                else:
                    half = L // 2
                    b_mid = b[half - 1:half, :]
                    d = jnp.concatenate([b[0:half, :], b[half:L, :] - b_mid], axis=0)
                    qf = qh * jnp.exp2(d)
                    qe = jnp.concatenate([qf[0:half, :], qf[half:L, :] * jnp.exp2(b_mid)], axis=0)
                    same = _dot_nt(qf, kh * jnp.exp2(-d))
                    cross = _dot_nt(qf, kh * jnp.exp2(b_mid - b))
                    pair.append(jnp.where(lev == HGRN_LEVELS - 1, cross, same))
                o_inter.append(_dot_nt(qe, st_ref[h]))
                upd.append(_dot_tn(vh, kh * jnp.exp2(b_last - b)))
                decay.append(jnp.exp2(b_last))
            outs = [o_inter[h] + _dot(jnp.where(lev >= 0, pair[h], 0.0), v[:, hcols[h]])
                    for h in heads]
            for h in heads:
                st_ref[h] = st_ref[h] * decay[h] + upd[h]
                o = outs[h]
                on = o * lax.rsqrt(jnp.mean(o * o, axis=-1, keepdims=True) + EPS) * nw
                o_ref[rows, hcols[h]] = (on * _silu(z[:, hcols[h]])).astype(o_ref.dtype)
            return carry

        def boundary_gap(b, l, cols):
            if l < _HGRN_MXU_LEVELS:
                return dall[(l + 1) * L:(l + 2) * L, cols]
            hw = 1 << l
            parts = []
            for s in range(0, L, 2 * hw):
                bp = b[s + hw - 1:s + hw, :]
                parts += [bp - b[s:s + hw, :], b[s + hw:s + 2 * hw, :] - bp]
            return jnp.concatenate(parts, axis=0)

        pair, o_inter, upd, decay = [], [], [], []
        for h in heads:
            cols = hcols[h]
            qh, kh, vh = q[:, cols], k[:, cols], v[:, cols]
            b = dall[0:L, cols]
            b_last = b[L - 1:L, :]
            o_inter.append(_dot_nt(qh * jnp.exp2(b), st_ref[h]))
            upd.append(_dot_tn(vh, kh * jnp.exp2(b_last - b)))
            decay.append(jnp.exp2(b_last))
            prods = [_dot_nt(qh, kh)]
            for l in range(HGRN_LEVELS):
                fac = jnp.exp2(boundary_gap(b, l, cols))
                prods.append(_dot_nt(qh * fac, kh * fac))
            pair.append(prods)
        outs = []
        for h in heads:
            a = jnp.where(lev == HGRN_LEVELS, pair[h][0], 0.0)
            for l in range(HGRN_LEVELS):
                a = jnp.where(lev == l, pair[h][l + 1], a)
            outs.append(o_inter[h] + _dot(a, v[:, hcols[h]]))
        for h in heads:
            st_ref[h] = st_ref[h] * decay[h] + upd[h]
            o = outs[h]
            on = o * lax.rsqrt(jnp.mean(o * o, axis=-1, keepdims=True) + EPS) * nw
            o_ref[rows, hcols[h]] = (on * _silu(z[:, hcols[h]])).astype(o_ref.dtype)
        return carry

    direct_ok = worst < HGRN_DIRECT_MAX_LOG2
    halves_ok = jnp.logical_and(jnp.logical_not(direct_ok), worst_half < HGRN_DIRECT_MAX_LOG2)

    @pl.when(direct_ok)
    def _():
        lax.fori_loop(0, n_chunks, functools.partial(body, path="direct"), 0, unroll=8)

    @pl.when(halves_ok)
    def _():
        lax.fori_loop(0, n_chunks, functools.partial(body, path="halves"), 0, unroll=8)

    @pl.when(jnp.logical_not(jnp.logical_or(direct_ok, halves_ok)))
    def _():
        lax.fori_loop(0, n_chunks, functools.partial(body, path="general"), 0, unroll=4)


def _hgrn(proj3, hgrn_lb, hgrn_norm_w, layer):
    bsz, seq, _ = proj3.shape
    w = BRANCH_WIDTH
    tb = 1024
    c_all, lev = _hgrn_constants()
    half = HGRN_CHUNK // 2
    n_ind = tb // half
    ind = np.zeros((2 * n_ind, tb), np.float32)
    for c in range(tb // HGRN_CHUNK):
        ind[c, c * HGRN_CHUNK:(c + 1) * HGRN_CHUNK] = 1.0
    for c in range(n_ind):
        ind[n_ind + c, c * half:(c + 1) * half] = 1.0
    col = lambda cb: pl.BlockSpec((None, tb, w), lambda b, s: (b, s, cb))
    return pl.pallas_call(
        functools.partial(_hgrn_kernel, layer=layer),
        out_shape=jax.ShapeDtypeStruct((bsz, seq, w), BF16),
        grid=(bsz, seq // tb),
        in_specs=[
            col(_COL_HGRN), col(_COL_HGRN + 1), col(_COL_HGRN + 2), col(_COL_HGRN + 3),
            pl.BlockSpec(hgrn_lb.shape, lambda b, s: (0, 0)),
            pl.BlockSpec((None, 1, HGRN_HEAD_DIM), lambda b, s: (layer, 0, 0)),
            pl.BlockSpec(c_all.shape, lambda b, s: (0, 0)),
            pl.BlockSpec(lev.shape, lambda b, s: (0, 0)),
            pl.BlockSpec(ind.shape, lambda b, s: (0, 0)),
        ],
        out_specs=pl.BlockSpec((None, tb, w), lambda b, s: (b, s, 0)),
        scratch_shapes=[pltpu.VMEM((HGRN_HEADS, HGRN_HEAD_DIM, HGRN_HEAD_DIM), F32),
                        pltpu.VMEM((tb, w), F32),
                        pltpu.VMEM((tb, w), F32)],
        compiler_params=pltpu.CompilerParams(
            dimension_semantics=("arbitrary", "arbitrary"),
            vmem_limit_bytes=_mib(32)),
        name="hgrn2",
    )(proj3, proj3, proj3, proj3, hgrn_lb,
      hgrn_norm_w.reshape(hgrn_norm_w.shape[0], 1, HGRN_HEAD_DIM),
      jnp.asarray(c_all, BF16), jnp.asarray(lev), jnp.asarray(ind, BF16))


_CONV_ROWS = 32
_CONV_GLU_ROWS = 64


def _conv_kernel(a_ref, g_ref, z_ref, w_ref, b_ref, lnw_ref, lnb_ref, o_ref, u_scr, wb_scr):
    ts = a_ref.shape[0]
    sub = SUBLANES_V7X

    @pl.when(pl.program_id(1) == 0)
    def _():
        u_scr[:, 0:CONV_HALO, :] = jnp.zeros((sub, CONV_HALO, u_scr.shape[2]), F32)

    @pl.when(pl.program_id(1) > 0)
    def _():
        for res in range(sub):
            u_scr[res, 0:CONV_HALO - res, :] = u_scr[res, ts:ts + CONV_HALO - res, :]

    for c in range(ts // _CONV_GLU_ROWS):
        r0 = c * _CONV_GLU_ROWS
        u = (a_ref[r0:r0 + _CONV_GLU_ROWS, :].astype(F32)
             * jax.nn.sigmoid(g_ref[r0:r0 + _CONV_GLU_ROWS, :].astype(F32)))
        for res in range(sub):
            u_scr[res, CONV_HALO - res + r0:CONV_HALO - res + r0 + _CONV_GLU_ROWS, :] = u

    wd = w_ref.shape[1]
    for j in range(CONV_WIDTH):
        wb_scr[j] = jnp.broadcast_to(w_ref[j:j + 1, :], (sub, wd))
    wb_scr[CONV_WIDTH] = jnp.broadcast_to(b_ref[...], (sub, wd))
    lnw = lnw_ref[...]
    lnb = lnb_ref[...]
    first = CONV_HALO - (CONV_WIDTH - 1)
    tiles = _CONV_ROWS // sub

    def chunk(c, carry):
        r0 = pl.multiple_of(c * _CONV_ROWS, _CONV_ROWS)
        acc = jnp.broadcast_to(wb_scr[CONV_WIDTH][None], (tiles, sub, wd))
        for j in range(CONV_WIDTH):
            res = (first + j) % sub
            win = u_scr[res, pl.ds(r0 + (first + j - res), _CONV_ROWS), :]
            acc = acc + wb_scr[j][None] * win.reshape(tiles, sub, wd)
        acc = acc.reshape(_CONV_ROWS, wd)
        mu = jnp.mean(acc, axis=-1, keepdims=True)
        d = acc - mu
        var = jnp.mean(d * d, axis=-1, keepdims=True)
        y = _silu(d * lax.rsqrt(var + EPS) * lnw + lnb)
        z = z_ref[pl.ds(r0, _CONV_ROWS), :].astype(F32)
        o_ref[pl.ds(r0, _CONV_ROWS), :] = (y * _silu(z)).astype(o_ref.dtype)
        return carry

    lax.fori_loop(0, ts // _CONV_ROWS, chunk, 0, unroll=8)


def _conv(proj3, conv_w, conv_b, ln_w, ln_b, layer):
    bsz, seq, _ = proj3.shape
    w = BRANCH_WIDTH
    ts = 512
    col = lambda cb: pl.BlockSpec((None, ts, w), lambda b, s: (b, s, cb))
    vec = lambda: pl.BlockSpec((None, 1, w), lambda b, s: (layer, 0, 0))
    depth = conv_w.shape[0]
    return pl.pallas_call(
        _conv_kernel,
        out_shape=jax.ShapeDtypeStruct((bsz, seq, w), BF16),
        grid=(bsz, seq // ts),
        in_specs=[
            col(_COL_CONV), col(_COL_CONV + 1), col(_COL_CONV + 2),
            pl.BlockSpec((None, CONV_WIDTH, w), lambda b, s: (layer, 0, 0)),
            vec(), vec(), vec(),
        ],
        out_specs=pl.BlockSpec((None, ts, w), lambda b, s: (b, s, 0)),
        scratch_shapes=[pltpu.VMEM((SUBLANES_V7X, CONV_HALO + ts, w), F32),
                        pltpu.VMEM((CONV_WIDTH + 1, SUBLANES_V7X, w), F32)],
        compiler_params=pltpu.CompilerParams(
            dimension_semantics=("arbitrary", "arbitrary"),
            vmem_limit_bytes=_mib(32)),
        name="conv_module",
    )(proj3, proj3, proj3, conv_w, conv_b.reshape(depth, 1, w), ln_w.reshape(depth, 1, w),
      ln_b.reshape(depth, 1, w))


def _sb_constants():
    n = SB_BLOCK
    u = (np.arange(n)[:, None] >= np.arange(n)[None, :]).astype(np.float32)
    return np.concatenate([u, np.ones((n, n), np.float32)], axis=1)


def _sb_kernel(q_ref, k_ref, v_ref, z_ref, u_ref, o_ref, q_scr, c_scr, o_scr, cmin_ref):
    n = SB_BLOCK
    pairs = q_ref.shape[1] // n
    prs = range(pairs)
    i = pl.program_id(1)
    lane = lax.broadcasted_iota(jnp.int32, (2 * n, n), 1)
    row = lax.broadcasted_iota(jnp.int32, (2 * n, n), 0)
    strict = lane < (row & (n - 1))
    head_a = lax.broadcasted_iota(jnp.int32, (n, n), 1) < SB_HEAD_DIM

    for p in prs:
        q = q_ref[:, p * n:(p + 1) * n].astype(F32) * (SB_HEAD_DIM ** -0.5)
        q_scr[p, 0:n, :] = jnp.where(head_a, q, 0.0).astype(BF16)
        q_scr[p, n:2 * n, :] = jnp.where(head_a, 0.0, q).astype(BF16)

    def visit(blocks, fresh):
        units = [(p, t) for p in prs for t in range(len(blocks))]
        kv = {}
        for t, (j, _) in enumerate(blocks):
            rows = pl.ds(pl.multiple_of(j * n, n), n)
            for p in prs:
                kv[p, t] = (k_ref[rows, p * n:(p + 1) * n], v_ref[rows, p * n:(p + 1) * n])
        logits = {u: _dot_nt(q_scr[u[0]], kv[u][0]) for u in units}
        sums = {}
        for u in units:
            p, t = u
            x = logits[u]
            drop = jnp.maximum(x, 0.0) + jnp.log(1.0 + jnp.exp2(jnp.abs(x) * (-LOG2E)))
            if blocks[t][1]:
                drop = jnp.where(strict, drop, 0.0)
            sums[u] = _dot(drop, u_ref[...])
        cmin = None
        pvs = []
        for p in prs:
            c = None if fresh else c_scr[p]
            pv = None
            for t in range(len(blocks)):
                r = sums[p, t]
                total = r[:, :n] if c is None else r[:, :n] + c
                w = jnp.exp(logits[p, t] - total)
                if blocks[t][1]:
                    w = jnp.where(strict, w, 0.0)
                term = _dot(w, kv[p, t][1])
                pv = term if pv is None else pv + term
                c = r[:, n:] if c is None else c + r[:, n:]
            c_scr[p] = c
            pvs.append(pv)
            cmin = c if cmin is None else jnp.minimum(cmin, c)
        for p in prs:
            o_scr[p] = pvs[p] if fresh else o_scr[p] + pvs[p]
        return jnp.min(cmin)

    @pl.when(i == 0)
    def _():
        cmin_ref[0] = visit([(i, True)], True)

    @pl.when(i == 1)
    def _():
        cmin_ref[0] = visit([(i, True), (i - 1, False)], True)

    @pl.when(i >= 2)
    def _():
        cmin_ref[0] = visit([(i, True), (i - 1, False), (i - 2, False)], True)

    def cond(carry):
        j, cmin = carry
        return jnp.logical_and(j >= 0, cmin < -SB_SKIP_LOG)

    def body(carry):
        j, _ = carry
        return j - 1, visit([(j, False)], False)

    lax.while_loop(cond, body, (i - 3, cmin_ref[0]))
    for p in range(pairs):
        o = jnp.where(head_a, o_scr[p, 0:n, :], o_scr[p, n:2 * n, :])
        cols = slice(p * n, (p + 1) * n)
        o_ref[:, cols] = (o * _silu(z_ref[:, cols].astype(F32))).astype(o_ref.dtype)


def _sb_attn(proj3):
    bsz, seq, _ = proj3.shape
    n = SB_BLOCK
    w = BRANCH_WIDTH
    base = _COL_SB
    u2 = _sb_constants()
    return pl.pallas_call(
        _sb_kernel,
        out_shape=jax.ShapeDtypeStruct((bsz, seq, w), BF16),
        grid=(bsz, seq // n),
        in_specs=[
            pl.BlockSpec((None, n, w), lambda b, i: (b, i, base)),
            pl.BlockSpec((None, seq, w), lambda b, i: (b, 0, base + 1)),
            pl.BlockSpec((None, seq, w), lambda b, i: (b, 0, base + 2)),
            pl.BlockSpec((None, n, w), lambda b, i: (b, i, base + 3)),
            pl.BlockSpec(u2.shape, lambda b, i: (0, 0)),
        ],
        out_specs=pl.BlockSpec((None, n, w), lambda b, i: (b, i, 0)),
        scratch_shapes=[pltpu.VMEM((w // n, 2 * n, n), BF16),
                        pltpu.VMEM((w // n, 2 * n, n), F32),
                        pltpu.VMEM((w // n, 2 * n, n), F32),
                        pltpu.SMEM((1,), F32)],
        compiler_params=pltpu.CompilerParams(
            dimension_semantics=("arbitrary", "arbitrary"),
            vmem_limit_bytes=_mib(40)),
        name="stick_breaking",
    )(proj3, proj3, proj3, proj3, jnp.asarray(u2, BF16))


def _merge_kernel(ya_ref, yb_ref, yc_ref, g0, g1, g2, g3, g4, g5, x_ref, gate_ref, wb_ref,
                  wo_ref, fnw_ref, o_ref, *, final):
    w = BRANCH_WIDTH
    ys = (ya_ref[...], yb_ref[...], yc_ref[...])
    gl = ((g0, g1), (g2, g3), (g4, g5))
    halves = []
    for half in range(2):
        acc = None
        for nb in range(N_BRANCH):
            br = jnp.dot(ys[nb], wb_ref[nb, :, half * w:(half + 1) * w],
                         preferred_element_type=F32)
            term = jax.nn.sigmoid(gl[nb][half][...].astype(F32)) * br
            acc = term if acc is None else acc + term
        halves.append(acc.astype(BF16))
    merged = jnp.concatenate(halves, axis=1)
    out = jnp.dot(merged, wo_ref[...], preferred_element_type=F32)
    xn = x_ref[...] + gate_ref[...] * out
    if final:
        ms = jnp.mean(xn * xn, axis=-1, keepdims=True)
        xn = xn * lax.rsqrt(ms + EPS) * fnw_ref[...]
    o_ref[...] = xn


def _merge(ya, yb, yc, proj, x2, mod5, wb_bf16, wo_bf16, final_norm_w, layer, seq, final):
    m, d = x2.shape
    w = BRANCH_WIDTH
    tm = 512
    per_seq = seq // tm
    gate0 = _COL_GATE
    ycol = lambda: pl.BlockSpec((tm, w), lambda i: (i, 0))
    gcol = lambda cb: pl.BlockSpec((tm, w), lambda i: (i, gate0 + cb))
    return pl.pallas_call(
        functools.partial(_merge_kernel, final=final),
        out_shape=jax.ShapeDtypeStruct((m, d), F32),
        grid=(m // tm,),
        in_specs=[
            ycol(), ycol(), ycol(),
            gcol(0), gcol(1), gcol(2), gcol(3), gcol(4), gcol(5),
            pl.BlockSpec((tm, d), lambda i: (i, 0)),
            pl.BlockSpec((None, None, None, 1, d), lambda i: (layer, i // per_seq, 2, 0, 0)),
            pl.BlockSpec((None, N_BRANCH, w, d), lambda i: (layer, 0, 0, 0)),
            pl.BlockSpec((None, d, d), lambda i: (layer, 0, 0)),
            pl.BlockSpec((1, d), lambda i: (0, 0)),
        ],
        out_specs=pl.BlockSpec((tm, d), lambda i: (i, 0)),
        compiler_params=pltpu.CompilerParams(
            dimension_semantics=("arbitrary",),
            vmem_limit_bytes=_mib(48)),
        name="merge_out",
    )(ya, yb, yc, proj, proj, proj, proj, proj, proj, x2, mod5, wb_bf16, wo_bf16,
      final_norm_w.reshape(1, d))


def kernel(x, c, ada_w, ada_b, norm_w, w_in, hgrn_lb, hgrn_norm_w, conv_w, conv_b, conv_ln_w,
           conv_ln_b, w_branch, w_out, final_norm_w):
    bsz, seq, d = x.shape
    depth = ada_w.shape[0]
    m = bsz * seq
    mod = _ada_mod(c, ada_w, ada_b)
    mod5 = mod.reshape(depth, bsz, 3, 1, d)
    w_in_b = w_in.astype(BF16)
    wb_b = w_branch.astype(BF16)
    wo_b = w_out.astype(BF16)
    x2 = x.reshape(m, d)
    for layer in range(depth):
        proj = _inproj(x2, mod5, norm_w, w_in_b, layer, seq)
        proj3 = proj.reshape(bsz, seq, proj.shape[-1])
        ya = _hgrn(proj3, hgrn_lb, hgrn_norm_w, layer)
        yb =_conv(proj3, conv_w, conv_b, conv_ln_w, conv_ln_b, layer)
        yc = _sb_attn(proj3)
        x2 = _merge(ya.reshape(m, -1), yb.reshape(m, -1), yc.reshape(m, -1), proj, x2, mod5,
                    wb_b, wo_b, final_norm_w, layer, seq, final=(layer == depth - 1))
    return x2.reshape(bsz, seq, d)
```
